```python
import jax, jax.numpy as jnp
from jax import lax
import numpy as np

D_MODEL = 2048
BATCH = 2
SEQ = 16384
DEPTH = 2

RW_HEADS = 8
RW_HEAD_DIM = 64
RW_WIDTH = RW_HEADS * RW_HEAD_DIM
RW_DECAY_LORA = 32
RW_AAA_LORA = 32
RW_MV_LORA = 32
RW_GATE_LORA = 96
RW_GN_EPS = 64e-5
GLA_HEADS = 4
GLA_DK = 64
GLA_DV = 128
GLA_GATE_LORA = 16
GLA_TAU = 16.0
GLA_CHUNK = 64
NSA_HEADS = 8
NSA_GROUPS = 2
NSA_HEAD_DIM = 64
NSA_WIDTH = NSA_HEADS * NSA_HEAD_DIM
NSA_KV_WIDTH = NSA_GROUPS * NSA_HEAD_DIM
CMP_LEN = 32
CMP_STRIDE = 16
SEL_LEN = 64
SEL_TOPK = 16
WINDOW = 512
Q_BLOCK = 128
ROPE_THETA = 10000.0
N_BRANCH = 3
FFN_DENSE = 5632
N_EXPERTS = 8
TOP_K = 2
FFN_EXPERT = 7168
NORM_EPS = 1e-6

GLA_COLS = 2 * GLA_HEADS * GLA_DK + 2 * GLA_HEADS * GLA_DV + GLA_GATE_LORA
NSA_COLS = NSA_WIDTH + 6 * NSA_KV_WIDTH + 3 * NSA_HEADS

kernel_name = 'hybrid_rwkv7_gla_nsa_moe_block'


def rwkv_cols(has_vres):
    return 3 * RW_WIDTH + RW_DECAY_LORA + RW_AAA_LORA + RW_GATE_LORA + (RW_MV_LORA if has_vres else 0)


def in_cols(layer):
    return N_BRANCH * D_MODEL + rwkv_cols(layer > 0) + GLA_COLS + NSA_COLS


def split_last(z, sizes):
    return jnp.split(z, [int(s) for s in np.cumsum(sizes)[:-1]], axis=-1)


def rms_norm(x, g, eps=NORM_EPS):
    xf = x.astype(jnp.float32)
    y = xf * lax.rsqrt(jnp.mean(xf * xf, axis=-1, keepdims=True) + eps)
    return (y * g.astype(jnp.float32)).astype(x.dtype)


def masked_softmax(s, mask):
    s = jnp.where(mask, s.astype(jnp.float32), -jnp.inf)
    m = jnp.max(s, axis=-1, keepdims=True)
    m = jnp.where(jnp.isfinite(m), m, 0.0)
    e = jnp.exp(s - m)
    return e / jnp.maximum(jnp.sum(e, axis=-1, keepdims=True), jnp.finfo(jnp.float32).tiny)


def rope_tables(pos, dim):
    inv = 1.0 / (ROPE_THETA ** (jnp.arange(0, dim, 2, dtype=jnp.float32) / dim))
    ang = pos.astype(jnp.float32)[:, None] * inv[None, :]
    return jnp.cos(ang), jnp.sin(ang)


def apply_rope(x, cos, sin):
    half = x.shape[-1] // 2
    x1 = x[..., :half].astype(jnp.float32)
    x2 = x[..., half:].astype(jnp.float32)
    c = cos[:, None, :]
    s = sin[:, None, :]
    return jnp.concatenate([x1 * c - x2 * s, x2 * c + x1 * s], axis=-1).astype(x.dtype)


def token_shift(z):
    return jnp.pad(z, ((0, 0), (1, 0), (0, 0)))[:, :-1]


def rwkv7_scan(r, w, k, v, kk, a):
    def step(state, inp):
        r_t, w_t, k_t, v_t, kk_t, a_t = inp
        sa = jnp.einsum('bhij,bhj->bhi', state, -kk_t)
        state = (state * w_t[:, :, None, :] + sa[..., None] * (kk_t * a_t)[:, :, None, :]
                 + v_t[..., None] * k_t[:, :, None, :])
        return state, jnp.einsum('bhij,bhj->bhi', state, r_t)
    B, S, H, N = r.shape
    xs = tuple(jnp.moveaxis(t.astype(jnp.float32), 1, 0) for t in (r, w, k, v, kk, a))
    _, ys = lax.scan(step, jnp.zeros((B, H, N, N), jnp.float32), xs)
    return jnp.moveaxis(ys, 0, 1)


def rwkv7_mix(z, p, v_first):
    B, S, _ = z.shape
    has_vres = v_first is not None
    z = z + (token_shift(z) - z) * p['rw_mu']
    sizes = [RW_WIDTH] * 3 + [RW_DECAY_LORA, RW_AAA_LORA, RW_GATE_LORA] + ([RW_MV_LORA] if has_vres else [])
    parts = split_last(z, sizes)
    r, k, v, zw, za, zg = parts[:6]
    w_log = -jax.nn.softplus(-(p['rw_w0'] + jnp.tanh(zw) @ p['rw_w2'])) - 0.5
    decay = jnp.exp(-jnp.exp(w_log.astype(jnp.float32)))
    a = jax.nn.sigmoid(p['rw_a0'] + za @ p['rw_a2'])
    g = jax.nn.sigmoid(zg) @ p['rw_g2']
    if has_vres:
        v = v + (v_first - v) * jax.nn.sigmoid(p['rw_v0'] + parts[6] @ p['rw_v2'])
    else:
        v_first = v

    def heads(t):
        return t.reshape(B, S, RW_HEADS, RW_HEAD_DIM)

    kk = heads(k * p['rw_kk']).astype(jnp.float32)
    kk = kk / jnp.maximum(jnp.sqrt(jnp.sum(kk * kk, axis=-1, keepdims=True)), 1e-12)
    k = k * (1.0 + (a - 1.0) * p['rw_ka'])
    rh, kh, vh = heads(r), heads(k), heads(v)
    y = rwkv7_scan(rh, heads(decay), kh, vh, kk, heads(a))
    mu = jnp.mean(y, axis=-1, keepdims=True)
    var = jnp.mean(jnp.square(y - mu), axis=-1, keepdims=True)
    y = ((y - mu) * lax.rsqrt(var + RW_GN_EPS) * p['rw_ln_w'].reshape(RW_HEADS, RW_HEAD_DIM)
         + p['rw_ln_b'].reshape(RW_HEADS, RW_HEAD_DIM))
    bonus = jnp.sum(rh * kh * p['rw_rk'], axis=-1, keepdims=True) * vh
    out = (y + bonus).reshape(B, S, RW_WIDTH) * g
    return out.astype(z.dtype), v_first


def gla_chunked(q, k, v, log_a):
    B, S, H, DK = q.shape
    DV = v.shape[-1]
    C = GLA_CHUNK
    N = S // C
    q = q.astype(jnp.float32).reshape(B, N, C, H, DK)
    k = k.astype(jnp.float32).reshape(B, N, C, H, DK)
    v = v.astype(jnp.float32).reshape(B, N, C, H, DV)
    b = jnp.cumsum(log_a.reshape(B, N, C, H, DK), axis=2)
    b_last = b[:, :, -1]
    q_dec = q * jnp.exp(b)
    k_inv = k * jnp.exp(-b)
    k_dec = k * jnp.exp(b_last[:, :, None] - b)
    causal = jnp.tril(jnp.ones((C, C), dtype=bool))
    att = jnp.where(causal, jnp.einsum('bnihd,bnjhd->bnhij', q_dec, k_inv), 0.0)
    o_intra = jnp.einsum('bnhij,bnjhv->bnihv', att, v)
    u = jnp.einsum('bnchd,bnchv->bnhdv', k_dec, v)

    def step(state, inp):
        u_n, dec_n = inp
        return dec_n[..., None] * state + u_n, state
    _, s_prev = lax.scan(step, jnp.zeros((B, H, DK, DV), jnp.float32),
                         (jnp.moveaxis(u, 1, 0), jnp.moveaxis(jnp.exp(b_last), 1, 0)))
    s_prev = jnp.moveaxis(s_prev, 0, 1)
    o_inter = jnp.einsum('bnihd,bnhdv->bnihv', q_dec, s_prev)
    return (o_intra + o_inter).reshape(B, S, H, DV)


def gla_mix(z, p):
    B, S, _ = z.shape
    q, k, v, zg, zo = split_last(z, [GLA_HEADS * GLA_DK, GLA_HEADS * GLA_DK, GLA_HEADS * GLA_DV,
                                     GLA_GATE_LORA, GLA_HEADS * GLA_DV])
    q = q.reshape(B, S, GLA_HEADS, GLA_DK) * GLA_DK ** -0.5
    k = k.reshape(B, S, GLA_HEADS, GLA_DK)
    v = v.reshape(B, S, GLA_HEADS, GLA_DV)
    log_a = jax.nn.log_sigmoid((zg @ p['gla_wg'] + p['gla_bg']).astype(jnp.float32)) / GLA_TAU
    o = gla_chunked(q, k, v, log_a.reshape(B, S, GLA_HEADS, GLA_DK))
    o = rms_norm(o, p['gla_norm']).reshape(B, S, GLA_HEADS * GLA_DV)
    return (o * jax.nn.silu(zo)).astype(z.dtype)


def compress_blocks(t, pe, w1, w2):
    B, S, G, d = t.shape
    r = CMP_LEN // CMP_STRIDE
    n_chunk = S // CMP_STRIDE
    n_cmp = n_chunk - r + 1
    chunks = t.reshape(B, n_chunk, CMP_STRIDE, G, d)
    blocks = jnp.concatenate([chunks[:, i:i + n_cmp] for i in range(r)], axis=2)
    h = jax.nn.gelu(jnp.einsum('bnlgd,lde->bnge', blocks + pe[:, None, :], w1))
    return jnp.einsum('bnge,ed->bngd', h, w2)


def nsa_attention(q, kc, vc, c_end, ks, vs, kw, vw):
    B, S, H, d = q.shape
    G = ks.shape[2]
    R = H // G
    n_sel = S // SEL_LEN
    k_sel = min(SEL_TOPK, n_sel)
    n_qb = S // Q_BLOCK
    qg = q.reshape(B, S, G, R, d)
    sel_start = jnp.arange(n_sel) * SEL_LEN
    c_start = c_end - (CMP_LEN - 1)
    overlap = ((c_start[:, None] <= sel_start[None, :] + SEL_LEN - 1)
               & (c_end[:, None] >= sel_start[None, :])).astype(jnp.float32)
    ks_blk = ks.reshape(B, n_sel, SEL_LEN, G, d).transpose(0, 3, 1, 2, 4)
    vs_blk = vs.reshape(B, n_sel, SEL_LEN, G, d).transpose(0, 3, 1, 2, 4)
    kw_pad = jnp.pad(kw, ((0, 0), (WINDOW, 0), (0, 0), (0, 0)))
    vw_pad = jnp.pad(vw, ((0, 0), (WINDOW, 0), (0, 0), (0, 0)))
    bi = jnp.arange(B)[:, None, None, None]
    gi = jnp.arange(G)[None, :, None, None]
    blk = jnp.arange(n_sel)

    def block(iq):
        q0 = iq * Q_BLOCK
        t = q0 + jnp.arange(Q_BLOCK)
        qb = lax.dynamic_slice_in_dim(qg, q0, Q_BLOCK, axis=1)
        s_c = jnp.einsum('bqgrd,bngd->bgrqn', qb, kc)
        p_c = masked_softmax(s_c, c_end[None, :] <= t[:, None])
        o_c = jnp.einsum('bgrqn,bngd->bqgrd', p_c, vc)
        imp = jnp.einsum('bgrqn,nj->bgqj', p_c, overlap)
        cur = (t // SEL_LEN)[:, None]
        forced = (blk == 0) | (blk == cur) | (blk == cur - 1)
        valid = blk * SEL_LEN <= t[:, None]
        score = jnp.where(forced, jnp.inf, jnp.where(valid, imp, -jnp.inf))
        _, idx = lax.top_k(score, k_sel)
        kg = ks_blk[bi, gi, idx]
        vg = vs_blk[bi, gi, idx].reshape(B, G, Q_BLOCK, k_sel * SEL_LEN, d)
        tok = idx[..., None] * SEL_LEN + jnp.arange(SEL_LEN)
        mask_s = (tok <= t[:, None, None]).reshape(B, G, 1, Q_BLOCK, k_sel * SEL_LEN)
        s_s = jnp.einsum('bqgrd,bgqkld->bgrqkl', qb, kg).reshape(B, G, R, Q_BLOCK, k_sel * SEL_LEN)
        p_s = masked_softmax(s_s, mask_s)
        o_s = jnp.einsum('bgrqm,bgqmd->bqgrd', p_s, vg)
        kwb = lax.dynamic_slice_in_dim(kw_pad, q0, Q_BLOCK + WINDOW, axis=1)
        vwb = lax.dynamic_slice_in_dim(vw_pad, q0, Q_BLOCK + WINDOW, axis=1)
        kpos = q0 - WINDOW + jnp.arange(Q_BLOCK + WINDOW)
        dist = t[:, None] - kpos[None, :]
        mask_w = (dist >= 0) & (dist < WINDOW) & (kpos[None, :] >= 0)
        s_w = jnp.einsum('bqgrd,bkgd->bgrqk', qb, kwb)
        p_w = masked_softmax(s_w, mask_w)
        o_w = jnp.einsum('bgrqk,bkgd->bqgrd', p_w, vwb)
        return o_c, o_s, o_w

    outs = lax.map(block, jnp.arange(n_qb))
    return tuple(o.transpose(1, 0, 2, 3, 4, 5).reshape(B, S, H, d) for o in outs)


def nsa_mix(z, p, cos, sin):
    B, S, _ = z.shape
    q, kc, vc, ks, vs, kw, vw, zg = split_last(z, [NSA_WIDTH] + [NSA_KV_WIDTH] * 6 + [3 * NSA_HEADS])

    def kv(t):
        return t.reshape(B, S, NSA_GROUPS, NSA_HEAD_DIM)

    q = apply_rope(rms_norm(q.reshape(B, S, NSA_HEADS, NSA_HEAD_DIM), p['nsa_qnorm']), cos, sin) * NSA_HEAD_DIM ** -0.5
    ks = apply_rope(rms_norm(kv(ks), p['nsa_ksnorm']), cos, sin)
    kw = apply_rope(rms_norm(kv(kw), p['nsa_kwnorm']), cos, sin)
    kc = compress_blocks(kv(kc), p['nsa_pe_k'], p['nsa_c1_k'], p['nsa_c2_k'])
    vc = compress_blocks(kv(vc), p['nsa_pe_v'], p['nsa_c1_v'], p['nsa_c2_v'])
    c_end = jnp.arange(kc.shape[1]) * CMP_STRIDE + (CMP_LEN - 1)
    cos_c, sin_c = rope_tables(c_end, NSA_HEAD_DIM)
    kc = apply_rope(rms_norm(kc, p['nsa_kcnorm']), cos_c, sin_c)
    o_c, o_s, o_w = nsa_attention(q, kc, vc, c_end, ks, kv(vs), kw, kv(vw))
    g = jax.nn.sigmoid(zg.astype(jnp.float32)).reshape(B, S, NSA_HEADS, 3)
    o = g[..., 0:1] * o_c + g[..., 1:2] * o_s + g[..., 2:3] * o_w
    return o.reshape(B, S, NSA_WIDTH).astype(z.dtype)


def hybrid_mixer(xn, p, v_first, cos, sin):
    B, S, _ = xn.shape
    z = xn @ p['w_in']
    z_gate, z_rw, z_gla, z_nsa = split_last(z, [N_BRANCH * D_MODEL, rwkv_cols(v_first is not None), GLA_COLS, NSA_COLS])
    y_rw, v_first = rwkv7_mix(z_rw, p, v_first)
    y_gla = gla_mix(z_gla, p)
    y_nsa = nsa_mix(z_nsa, p, cos, sin)
    gate = jax.nn.sigmoid(z_gate).reshape(B, S, N_BRANCH, D_MODEL)
    merged = (gate[:, :, 0] * (y_rw @ p['proj_a']) + gate[:, :, 1] * (y_gla @ p['proj_b'])
              + gate[:, :, 2] * (y_nsa @ p['proj_c']))
    return merged @ p['w_out'], v_first


def swiglu(x, wg, wu, wd):
    return (jax.nn.silu(x @ wg) * (x @ wu)) @ wd


def moe_swiglu(x, router, eg, eu, ed):
    probs = jax.nn.softmax((x @ router).astype(jnp.float32), axis=-1)
    top_v, top_i = lax.top_k(probs, TOP_K)
    top_v = top_v / jnp.sum(top_v, axis=-1, keepdims=True)
    combine = jnp.sum(jax.nn.one_hot(top_i, N_EXPERTS, dtype=jnp.float32) * top_v[..., None], axis=-2)
    y = jnp.zeros(x.shape, jnp.float32)
    for e in range(N_EXPERTS):
        y = y + combine[..., e:e + 1] * swiglu(x, eg[e], eu[e], ed[e])
    return y.astype(x.dtype)


def _key_stream(key, n):
    ks = jax.random.split(key, n)
    return (ks[i] for i in range(n))


def setup_inputs(seed: int = 0) -> dict:
    key = jax.random.key(seed)
    keys = _key_stream(key, 256)

    def nrm(shape, scale):
        return scale * jax.random.normal(next(keys), shape, jnp.float32)

    def gain(n):
        return 1.0 + nrm((n,), 0.02)

    out = {'x': nrm((BATCH, SEQ, D_MODEL), 1.0)}
    for i in range(DEPTH):
        P = f'l{i}_'
        out[P + 'norm_mix'] = gain(D_MODEL)
        out[P + 'w_in'] = nrm((D_MODEL, in_cols(i)), D_MODEL ** -0.5)
        out[P + 'rw_mu'] = jax.random.uniform(next(keys), (rwkv_cols(i > 0),), jnp.float32)
        out[P + 'rw_w0'] = nrm((RW_WIDTH,), 1.0) - 1.0
        out[P + 'rw_w2'] = nrm((RW_DECAY_LORA, RW_WIDTH), 0.5 * RW_DECAY_LORA ** -0.5)
        out[P + 'rw_a0'] = nrm((RW_WIDTH,), 0.5)
        out[P + 'rw_a2'] = nrm((RW_AAA_LORA, RW_WIDTH), 0.5 * RW_AAA_LORA ** -0.5)
        out[P + 'rw_g2'] = nrm((RW_GATE_LORA, RW_WIDTH), RW_GATE_LORA ** -0.5)
        if i > 0:
            out[P + 'rw_v0'] = nrm((RW_WIDTH,), 0.5)
            out[P + 'rw_v2'] = nrm((RW_MV_LORA, RW_WIDTH), 0.5 * RW_MV_LORA ** -0.5)
        out[P + 'rw_kk'] = 0.85 + nrm((RW_WIDTH,), 0.02)
        out[P + 'rw_ka'] = gain(RW_WIDTH)
        out[P + 'rw_rk'] = nrm((RW_HEADS, RW_HEAD_DIM), 0.1)
        out[P + 'rw_ln_w'] = gain(RW_WIDTH)
        out[P + 'rw_ln_b'] = nrm((RW_WIDTH,), 0.02)
        out[P + 'gla_wg'] = nrm((GLA_GATE_LORA, GLA_HEADS * GLA_DK), GLA_GATE_LORA ** -0.5)
        out[P + 'gla_bg'] = nrm((GLA_HEADS * GLA_DK,), 0.02)
        out[P + 'gla_norm'] = gain(GLA_DV)
        out[P + 'nsa_qnorm'] = gain(NSA_HEAD_DIM)
        out[P + 'nsa_kcnorm'] = gain(NSA_HEAD_DIM)
        out[P + 'nsa_ksnorm'] = gain(NSA_HEAD_DIM)
        out[P + 'nsa_kwnorm'] = gain(NSA_HEAD_DIM)
        for t in ('k', 'v'):
            out[P + 'nsa_pe_' + t] = nrm((CMP_LEN, NSA_HEAD_DIM), 0.02)
            out[P + 'nsa_c1_' + t] = nrm((CMP_LEN, NSA_HEAD_DIM, NSA_HEAD_DIM), (CMP_LEN * NSA_HEAD_DIM) ** -0.5)
            out[P + 'nsa_c2_' + t] = nrm((NSA_HEAD_DIM, NSA_HEAD_DIM), NSA_HEAD_DIM ** -0.5)
        out[P + 'proj_a'] = nrm((RW_WIDTH, D_MODEL), RW_WIDTH ** -0.5)
        out[P + 'proj_b'] = nrm((GLA_HEADS * GLA_DV, D_MODEL), (GLA_HEADS * GLA_DV) ** -0.5)
        out[P + 'proj_c'] = nrm((NSA_WIDTH, D_MODEL), NSA_WIDTH ** -0.5)
        out[P + 'w_out'] = nrm((D_MODEL, D_MODEL), D_MODEL ** -0.5)
        out[P + 'norm_ffn'] = gain(D_MODEL)
        if i % 2 == 0:
            out[P + 'ffn_gate'] = nrm((D_MODEL, FFN_DENSE), D_MODEL ** -0.5)
            out[P + 'ffn_up'] = nrm((D_MODEL, FFN_DENSE), D_MODEL ** -0.5)
            out[P + 'ffn_down'] = nrm((FFN_DENSE, D_MODEL), FFN_DENSE ** -0.5)
        else:
            out[P + 'router'] = nrm((D_MODEL, N_EXPERTS), D_MODEL ** -0.5)
            out[P + 'exp_gate'] = nrm((N_EXPERTS, D_MODEL, FFN_EXPERT), D_MODEL ** -0.5)
            out[P + 'exp_up'] = nrm((N_EXPERTS, D_MODEL, FFN_EXPERT), D_MODEL ** -0.5)
            out[P + 'exp_down'] = nrm((N_EXPERTS, FFN_EXPERT, D_MODEL), FFN_EXPERT ** -0.5)
    return out


def reference(x,
              l0_norm_mix, l0_w_in, l0_rw_mu, l0_rw_w0, l0_rw_w2, l0_rw_a0, l0_rw_a2, l0_rw_g2,
              l0_rw_kk, l0_rw_ka, l0_rw_rk, l0_rw_ln_w, l0_rw_ln_b,
              l0_gla_wg, l0_gla_bg, l0_gla_norm,
              l0_nsa_qnorm, l0_nsa_kcnorm, l0_nsa_ksnorm, l0_nsa_kwnorm,
              l0_nsa_pe_k, l0_nsa_c1_k, l0_nsa_c2_k, l0_nsa_pe_v, l0_nsa_c1_v, l0_nsa_c2_v,
              l0_proj_a, l0_proj_b, l0_proj_c, l0_w_out, l0_norm_ffn,
              l0_ffn_gate, l0_ffn_up, l0_ffn_down,
              l1_norm_mix, l1_w_in, l1_rw_mu, l1_rw_w0, l1_rw_w2, l1_rw_a0, l1_rw_a2, l1_rw_g2,
              l1_rw_v0, l1_rw_v2, l1_rw_kk, l1_rw_ka, l1_rw_rk, l1_rw_ln_w, l1_rw_ln_b,
              l1_gla_wg, l1_gla_bg, l1_gla_norm,
              l1_nsa_qnorm, l1_nsa_kcnorm, l1_nsa_ksnorm, l1_nsa_kwnorm,
              l1_nsa_pe_k, l1_nsa_c1_k, l1_nsa_c2_k, l1_nsa_pe_v, l1_nsa_c1_v, l1_nsa_c2_v,
              l1_proj_a, l1_proj_b, l1_proj_c, l1_w_out, l1_norm_ffn,
              l1_router, l1_exp_gate, l1_exp_up, l1_exp_down):
    p0 = dict(norm_mix=l0_norm_mix, w_in=l0_w_in, rw_mu=l0_rw_mu, rw_w0=l0_rw_w0, rw_w2=l0_rw_w2,
              rw_a0=l0_rw_a0, rw_a2=l0_rw_a2, rw_g2=l0_rw_g2, rw_kk=l0_rw_kk, rw_ka=l0_rw_ka,
              rw_rk=l0_rw_rk, rw_ln_w=l0_rw_ln_w, rw_ln_b=l0_rw_ln_b,
              gla_wg=l0_gla_wg, gla_bg=l0_gla_bg, gla_norm=l0_gla_norm,
              nsa_qnorm=l0_nsa_qnorm, nsa_kcnorm=l0_nsa_kcnorm, nsa_ksnorm=l0_nsa_ksnorm,
              nsa_kwnorm=l0_nsa_kwnorm, nsa_pe_k=l0_nsa_pe_k, nsa_c1_k=l0_nsa_c1_k, nsa_c2_k=l0_nsa_c2_k,
              nsa_pe_v=l0_nsa_pe_v, nsa_c1_v=l0_nsa_c1_v, nsa_c2_v=l0_nsa_c2_v,
              proj_a=l0_proj_a, proj_b=l0_proj_b, proj_c=l0_proj_c, w_out=l0_w_out,
              norm_ffn=l0_norm_ffn, ffn_gate=l0_ffn_gate, ffn_up=l0_ffn_up, ffn_down=l0_ffn_down)
    p1 = dict(norm_mix=l1_norm_mix, w_in=l1_w_in, rw_mu=l1_rw_mu, rw_w0=l1_rw_w0, rw_w2=l1_rw_w2,
              rw_a0=l1_rw_a0, rw_a2=l1_rw_a2, rw_g2=l1_rw_g2, rw_v0=l1_rw_v0, rw_v2=l1_rw_v2,
              rw_kk=l1_rw_kk, rw_ka=l1_rw_ka, rw_rk=l1_rw_rk, rw_ln_w=l1_rw_ln_w, rw_ln_b=l1_rw_ln_b,
              gla_wg=l1_gla_wg, gla_bg=l1_gla_bg, gla_norm=l1_gla_norm,
              nsa_qnorm=l1_nsa_qnorm, nsa_kcnorm=l1_nsa_kcnorm, nsa_ksnorm=l1_nsa_ksnorm,
              nsa_kwnorm=l1_nsa_kwnorm, nsa_pe_k=l1_nsa_pe_k, nsa_c1_k=l1_nsa_c1_k, nsa_c2_k=l1_nsa_c2_k,
              nsa_pe_v=l1_nsa_pe_v, nsa_c1_v=l1_nsa_c1_v, nsa_c2_v=l1_nsa_c2_v,
              proj_a=l1_proj_a, proj_b=l1_proj_b, proj_c=l1_proj_c, w_out=l1_w_out,
              norm_ffn=l1_norm_ffn, router=l1_router, exp_gate=l1_exp_gate, exp_up=l1_exp_up,
              exp_down=l1_exp_down)
    layers = (p0, p1)
    cos, sin = rope_tables(jnp.arange(x.shape[1]), NSA_HEAD_DIM)
    h = x
    v_first = None
    for i in range(DEPTH):
        p = layers[i]
        y, v_first = hybrid_mixer(rms_norm(h, p['norm_mix']), p, v_first, cos, sin)
        h = h + y
        hn = rms_norm(h, p['norm_ffn'])
        if i % 2 == 0:
            h = h + swiglu(hn, p['ffn_gate'], p['ffn_up'], p['ffn_down'])
        else:
            h = h + moe_swiglu(hn, p['router'], p['exp_gate'], p['exp_up'], p['exp_down'])
    return h
```

```python
import functools

import numpy as np
import jax
import jax.numpy as jnp
from jax import lax
from jax.experimental import pallas as pl
from jax.experimental.pallas import tpu as pltpu

F32 = jnp.float32
BF16 = jnp.bfloat16
HI = lax.Precision.HIGHEST

D_MODEL = 2048
RW_HEADS = 8
RW_HEAD_DIM = 64
RW_WIDTH = 512
RW_GN_EPS = 64e-5
GLA_HEADS = 4
GLA_DK = 64
GLA_DV = 128
GLA_TAU = 16.0
CHUNK = 64
NSA_HEADS = 8
NSA_GROUPS = 2
NSA_REP = NSA_HEADS // NSA_GROUPS
NSA_HEAD_DIM = 64
CMP_LEN = 32
CMP_STRIDE = 16
SEL_LEN = 64
SEL_TOPK = 16
WINDOW = 512
Q_BLOCK = 128
ROPE_THETA = 10000.0
N_EXPERTS = 8
TOP_K = 2
NORM_EPS = 1e-6

LANES = 128
VMEM_LIMIT = 56 * 1024 * 1024

SEG = 1536
SMALL = 256
SMALL_OFF = 3 * SEG - SMALL
S_ZW, S_ZA, S_ZG, S_ZV, S_GLA, S_NSA = 0, 32, 64, 160, 192, 208
NEG = -1e30


def _dot(a, b, precision=None):
    return jnp.dot(a, b, preferred_element_type=F32, precision=precision)


def _dot_nt(a, b, precision=None):
    return lax.dot_general(a, b, (((1,), (1,)), ((), ())), preferred_element_type=F32, precision=precision)


def _dot_tn(a, b, precision=None):
    return lax.dot_general(a, b, (((0,), (0,)), ((), ())), preferred_element_type=F32, precision=precision)


def _cparams(*sem):
    return pltpu.CompilerParams(dimension_semantics=sem, vmem_limit_bytes=VMEM_LIMIT)


def _full(shape):
    n = len(shape)
    return pl.BlockSpec(shape, lambda *_: (0,) * n)


def _norm_matmul_body(x_ref, g_ref, w_ref, o_ref, xn_ref):
    @pl.when(pl.program_id(1) == 0)
    def _():
        x = x_ref[...]
        ms = jnp.mean(x * x, axis=-1, keepdims=True)
        xn_ref[...] = (x * lax.rsqrt(ms + NORM_EPS) * g_ref[...]).astype(BF16)

    o_ref[...] = _dot(xn_ref[...], w_ref[...]).astype(o_ref.dtype)


def norm_matmul(x, g, w, out_dtype, tm, tn):
    M, K = x.shape
    N = w.shape[1]
    return pl.pallas_call(
        _norm_matmul_body,
        grid=(M // tm, N // tn),
        in_specs=[pl.BlockSpec((tm, K), lambda i, j: (i, 0)),
                  pl.BlockSpec((1, K), lambda i, j: (0, 0)),
                  pl.BlockSpec((K, tn), lambda i, j: (0, j))],
        out_specs=pl.BlockSpec((tm, tn), lambda i, j: (i, j)),
        out_shape=jax.ShapeDtypeStruct((M, N), out_dtype),
        scratch_shapes=[pltpu.VMEM((tm, K), BF16)],
        compiler_params=_cparams("parallel", "arbitrary"),
        name="norm_matmul",
    )(x, g.reshape(1, K), w)


def _shift_rows(x, prev_row):
    rolled = pltpu.roll(x, 1, axis=0)
    first = lax.broadcasted_iota(jnp.int32, x.shape, 0) == 0
    return jnp.where(first, prev_row, rolled)


def _rwkv_prep_body(*refs, tiles_per_seq, has_vres):
    if has_vres:
        (zm_ref, zs_ref, vf_ref, mum_ref, mus_ref, vec_ref, w2_ref, a2_ref, g2_ref, v2_ref, bd_ref,
         r_ref, lw_ref, k_ref, v_ref, kk_ref, b_ref, g_ref, bonus_ref, pm_ref, ps_ref) = refs
    else:
        (zm_ref, zs_ref, mum_ref, mus_ref, vec_ref, w2_ref, a2_ref, g2_ref, bd_ref,
         r_ref, lw_ref, k_ref, v_ref, kk_ref, b_ref, g_ref, bonus_ref, pm_ref, ps_ref) = refs
    i = pl.program_id(0)

    @pl.when(i % tiles_per_seq == 0)
    def _():
        pm_ref[...] = jnp.zeros_like(pm_ref)
        ps_ref[...] = jnp.zeros_like(ps_ref)

    zm = zm_ref[...]
    zs = zs_ref[...]
    tm = zm.shape[0]
    zm_prev = _shift_rows(zm, pm_ref[0:1, :])
    zs_prev = _shift_rows(zs, ps_ref[0:1, :])
    pm_ref[0:1, :] = zm[tm - 1:tm, :]
    ps_ref[0:1, :] = zs[tm - 1:tm, :]
    zm = zm + (zm_prev - zm) * mum_ref[...]
    zs = zs + (zs_prev - zs) * mus_ref[...]

    r = zm[:, 0:RW_WIDTH]
    k = zm[:, RW_WIDTH:2 * RW_WIDTH]
    v = zm[:, 2 * RW_WIDTH:3 * RW_WIDTH]
    w0 = vec_ref[0:1, :]
    a0 = vec_ref[1:2, :]
    v0 = vec_ref[2:3, :]
    kkw = vec_ref[3:4, :]
    ka = vec_ref[4:5, :]
    rk = vec_ref[5:6, :]

    w_log = -jax.nn.softplus(-(w0 + _dot(jnp.tanh(zs), w2_ref[...], HI))) - 0.5
    lw = -jnp.exp(w_log)
    a = jax.nn.sigmoid(a0 + _dot(zs, a2_ref[...], HI))
    g = _dot(jax.nn.sigmoid(zs), g2_ref[...], HI)
    if has_vres:
        v = v + (vf_ref[...] - v) * jax.nn.sigmoid(v0 + _dot(zs, v2_ref[...], HI))

    bd = bd_ref[...]
    kk = k * kkw
    nrm = jnp.sqrt(_dot(kk * kk, bd, HI))
    kk = kk / jnp.maximum(nrm, 1e-12)
    k2 = k * (1.0 + (a - 1.0) * ka)
    bonus = _dot(r * k2 * rk, bd, HI) * v

    r_ref[...] = r
    lw_ref[...] = lw
    k_ref[...] = k2
    v_ref[...] = v
    kk_ref[...] = kk
    b_ref[...] = kk * a
    g_ref[...] = g
    bonus_ref[...] = bonus


def _pad_rows(w, off, total=SMALL):
    return jnp.zeros((total, w.shape[1]), F32).at[off:off + w.shape[0]].set(w)


def rwkv_prep(z_rest, v_first, p, seq, tm=256):
    T = z_rest.shape[0]
    has_vres = v_first is not None
    mu = p['rw_mu']
    n_small = mu.shape[0] - 3 * RW_WIDTH
    mum = mu[:3 * RW_WIDTH].reshape(1, -1)
    mus = jnp.zeros((SMALL,), F32).at[:n_small].set(mu[3 * RW_WIDTH:]).reshape(1, -1)
    zero = jnp.zeros((RW_WIDTH,), F32)
    vec = jnp.stack([p['rw_w0'], p['rw_a0'], p['rw_v0'] if has_vres else zero, p['rw_kk'], p['rw_ka'],
                     p['rw_rk'].reshape(-1), zero, zero])
    w2 = _pad_rows(p['rw_w2'], S_ZW)
    a2 = _pad_rows(p['rw_a2'], S_ZA)
    g2 = _pad_rows(p['rw_g2'], S_ZG)
    hid = np.arange(RW_WIDTH) // RW_HEAD_DIM
    bd = jnp.asarray((hid[:, None] == hid[None, :]).astype(np.float32))
    row = lambda w: pl.BlockSpec((tm, w), lambda i: (i, 0))
    in_specs = [row(SEG), pl.BlockSpec((tm, SMALL), lambda i: (i, SMALL_OFF // SMALL))]
    args = [z_rest, z_rest]
    if has_vres:
        in_specs.append(row(RW_WIDTH))
        args.append(v_first)
    in_specs += [_full((1, SEG)), _full((1, SMALL)), _full((8, RW_WIDTH)),
                 _full((SMALL, RW_WIDTH)), _full((SMALL, RW_WIDTH)), _full((SMALL, RW_WIDTH))]
    args += [mum, mus, vec, w2, a2, g2]
    if has_vres:
        in_specs.append(_full((SMALL, RW_WIDTH)))
        args.append(_pad_rows(p['rw_v2'], S_ZV))
    in_specs.append(_full((RW_WIDTH, RW_WIDTH)))
    args.append(bd)
    out = jax.ShapeDtypeStruct((T, RW_WIDTH), F32)
    return pl.pallas_call(
        functools.partial(_rwkv_prep_body, tiles_per_seq=seq // tm, has_vres=has_vres),
        grid=(T // tm,),
        in_specs=in_specs,
        out_specs=[row(RW_WIDTH)] * 8,
        out_shape=[out] * 8,
        scratch_shapes=[pltpu.VMEM((8, SEG), F32), pltpu.VMEM((8, SMALL), F32)],
        compiler_params=_cparams("arbitrary"),
        name="rwkv_prep",
    )(*args)


def _unit_lower_inverse(L, eye, masks):
    T = eye
    for mk in masks:
        lo = jnp.where(mk, L, 0.0).astype(BF16)
        n = _dot(T.astype(BF16), lo)
        T = T - _dot(n.astype(BF16), T.astype(BF16))
    return T


def _rwkv_chunk_body(r_ref, lw_ref, k_ref, v_ref, kk_ref, b_ref, g_ref, bonus_ref, lnw_ref, lnb_ref,
                     o_ref, state_ref, *, n_chunks):
    C = CHUNK
    N = RW_HEAD_DIM

    @pl.when(pl.program_id(1) == 0)
    def _():
        state_ref[...] = jnp.zeros_like(state_ref)

    ti = lax.broadcasted_iota(jnp.int32, (C, C), 0)
    si = lax.broadcasted_iota(jnp.int32, (C, C), 1)
    incl = si <= ti
    strict = si < ti
    tril1 = incl.astype(F32)
    eye = (si == ti).astype(F32)
    masks = []
    m = 1
    while m < C:
        masks.append(((ti // (2 * m)) == (si // (2 * m))) & ((ti // m) % 2 == 1) & ((si // m) % 2 == 0))
        m *= 2
    lnw = lnw_ref[...]
    lnb = lnb_ref[...]

    def chunk(c, carry):
        rows = pl.ds(pl.multiple_of(c * C, C), C)
        lw = lw_ref[0, rows, :]
        r = r_ref[0, rows, :]
        k = k_ref[0, rows, :]
        v = v_ref[0, rows, :]
        kk = kk_ref[0, rows, :]
        b = b_ref[0, rows, :]
        g = g_ref[0, rows, :]
        bonus = bonus_ref[0, rows, :]
        cum = _dot(tril1, lw, HI)
        cum_last = cum[C - 1:C, :]
        e_neg = jnp.exp(-cum)
        e_rel = jnp.exp(cum_last - cum)
        alpha = (kk * jnp.exp(cum - lw)).astype(BF16)
        beta = (b * e_neg).astype(BF16)
        kappa = (k * e_neg).astype(BF16)
        rho = (r * jnp.exp(cum)).astype(BF16)
        beta2 = (b * e_rel).astype(BF16)
        kappa2 = (k * e_rel).astype(BF16)
        pc = jnp.exp(cum_last)
        vb = v.astype(BF16)
        for h in range(RW_HEADS):
            sl = slice(h * N, (h + 1) * N)
            A = alpha[:, sl]
            Rh = rho[:, sl]
            V = vb[:, sl]
            p4 = _dot_nt(jnp.concatenate([A, Rh], axis=0), jnp.concatenate([beta[:, sl], kappa[:, sl]], axis=0))
            L = jnp.where(strict, p4[:C, :C], 0.0)
            AK = jnp.where(strict, p4[:C, C:], 0.0)
            G = jnp.where(incl, p4[C:, :C], 0.0)
            GK = jnp.where(incl, p4[C:, C:], 0.0)
            T = _unit_lower_inverse(L, eye, masks).astype(BF16)
            kv = _dot(jnp.concatenate([AK, GK], axis=0).astype(BF16), V)
            a_hat = _dot(T, A)
            W = _dot(T, kv[:C].astype(BF16))
            D = _dot_tn(V, kappa2[:, sl])
            S = state_ref[h]
            rs = _dot_nt(jnp.concatenate([a_hat.astype(BF16), Rh], axis=0), S.astype(BF16))
            U = -(rs[:C] + W)
            Ub = U.astype(BF16)
            Y = rs[C:] + _dot(G.astype(BF16), Ub) + kv[C:]
            state_ref[h] = S * pc[:, sl] + _dot_tn(Ub, beta2[:, sl]) + D
            mu = jnp.mean(Y, axis=-1, keepdims=True)
            yc = Y - mu
            var = jnp.mean(yc * yc, axis=-1, keepdims=True)
            yn = yc * lax.rsqrt(var + RW_GN_EPS) * lnw[:, sl] + lnb[:, sl]
            o_ref[0, rows, sl] = (yn + bonus[:, sl]) * g[:, sl]
        return carry

    lax.fori_loop(0, n_chunks, chunk, 0)


def rwkv_chunk(prep, p, batch, seq, rows=256):
    r, lw, k2, v, kk, b, g, bonus = [t.reshape(batch, seq, RW_WIDTH) for t in prep]
    blk = pl.BlockSpec((1, rows, RW_WIDTH), lambda bi, i: (bi, i, 0))
    return pl.pallas_call(
        functools.partial(_rwkv_chunk_body, n_chunks=rows // CHUNK),
        grid=(batch, seq // rows),
        in_specs=[blk] * 8 + [_full((1, RW_WIDTH))] * 2,
        out_specs=blk,
        out_shape=jax.ShapeDtypeStruct((batch, seq, RW_WIDTH), F32),
        scratch_shapes=[pltpu.VMEM((RW_HEADS, RW_HEAD_DIM, RW_HEAD_DIM), F32)],
        compiler_params=_cparams("parallel", "arbitrary"),
        name="rwkv_chunk",
    )(r, lw, k2, v, kk, b, g, bonus, p['rw_ln_w'].reshape(1, -1), p['rw_ln_b'].reshape(1, -1))


def _gla_body(z_ref, zs_ref, wg_ref, bg_ref, gn_ref, o_ref, state_ref, *, n_chunks):
    C = CHUNK
    HK = GLA_HEADS * GLA_DK
    HV = GLA_HEADS * GLA_DV

    @pl.when(pl.program_id(1) == 0)
    def _():
        state_ref[...] = jnp.zeros_like(state_ref)

    ti = lax.broadcasted_iota(jnp.int32, (C, C), 0)
    si = lax.broadcasted_iota(jnp.int32, (C, C), 1)
    tril1 = (si <= ti).astype(F32)
    causal = (lax.broadcasted_iota(jnp.int32, (C, HK), 1) % GLA_DK) <= lax.broadcasted_iota(jnp.int32, (C, HK), 0)
    blk_k = (lax.broadcasted_iota(jnp.int32, (HK, HK), 0) // GLA_DK) == (lax.broadcasted_iota(jnp.int32, (HK, HK), 1) // GLA_DK)
    blk_v = (lax.broadcasted_iota(jnp.int32, (HK, HV), 0) // GLA_DK) == (lax.broadcasted_iota(jnp.int32, (HK, HV), 1) // GLA_DV)
    blk_vt = (lax.broadcasted_iota(jnp.int32, (HV, HK), 0) // GLA_DV) == (lax.broadcasted_iota(jnp.int32, (HV, HK), 1) // GLA_DK)

    def chunk(c, carry):
        rows = pl.ds(pl.multiple_of(c * C, C), C)
        z = z_ref[0, rows, :]
        zs = zs_ref[0, rows, :]
        q = z[:, 0:HK] * (GLA_DK ** -0.5)
        k = z[:, HK:2 * HK]
        v = z[:, 2 * HK:2 * HK + HV]
        zo = z[:, 2 * HK + HV:2 * HK + 2 * HV]
        log_a = jax.nn.log_sigmoid(_dot(zs, wg_ref[...], HI) + bg_ref[...]) / GLA_TAU
        bcum = _dot(tril1, log_a, HI)
        b_last = bcum[C - 1:C, :]
        q_dec = (q * jnp.exp(bcum)).astype(BF16)
        k_inv = (k * jnp.exp(-bcum)).astype(BF16)
        k_dec = (k * jnp.exp(b_last - bcum)).astype(BF16)
        vb = v.astype(BF16)
        kbd = jnp.where(blk_k, jnp.concatenate([k_inv] * GLA_HEADS, axis=0), jnp.zeros((), BF16))
        att = jnp.where(causal, _dot_nt(q_dec, kbd), 0.0)
        vbd = jnp.where(blk_v, jnp.concatenate([vb] * GLA_HEADS, axis=0), jnp.zeros((), BF16))
        st = state_ref[...]
        o = _dot(att.astype(BF16), vbd) + _dot_nt(q_dec, st.astype(BF16))
        state_ref[...] = st * jnp.exp(b_last) + jnp.where(blk_vt, _dot_tn(vb, k_dec), 0.0)
        parts = []
        for h in range(GLA_HEADS):
            oh = o[:, h * GLA_DV:(h + 1) * GLA_DV]
            ms = jnp.mean(oh * oh, axis=-1, keepdims=True)
            parts.append(oh * lax.rsqrt(ms + NORM_EPS))
        on = jnp.concatenate(parts, axis=-1) * gn_ref[...]
        o_ref[0, rows, :] = on * jax.nn.silu(zo)
        return carry

    lax.fori_loop(0, n_chunks, chunk, 0)


def gla_mix(z_rest3, p, rows=256):
    batch, seq, _ = z_rest3.shape
    HV = GLA_HEADS * GLA_DV
    wg = jnp.zeros((SMALL, GLA_HEADS * GLA_DK), F32).at[S_GLA:S_GLA + p['gla_wg'].shape[0]].set(p['gla_wg'])
    return pl.pallas_call(
        functools.partial(_gla_body, n_chunks=rows // CHUNK),
        grid=(batch, seq // rows),
        in_specs=[pl.BlockSpec((1, rows, SEG), lambda b, i: (b, i, 1)),
                  pl.BlockSpec((1, rows, SMALL), lambda b, i: (b, i, SMALL_OFF // SMALL)),
                  _full((SMALL, GLA_HEADS * GLA_DK)), _full((1, GLA_HEADS * GLA_DK)), _full((1, HV))],
        out_specs=pl.BlockSpec((1, rows, HV), lambda b, i: (b, i, 0)),
        out_shape=jax.ShapeDtypeStruct((batch, seq, HV), F32),
        scratch_shapes=[pltpu.VMEM((HV, GLA_HEADS * GLA_DK), F32)],
        compiler_params=_cparams("parallel", "arbitrary"),
        name="gla_mix",
    )(z_rest3, z_rest3, wg, p['gla_bg'].reshape(1, -1), jnp.tile(p['gla_norm'], GLA_HEADS).reshape(1, -1))


def _rope(x, cos, sin_signed):
    w = x.shape[-1]
    lane = lax.broadcasted_iota(jnp.int32, x.shape, 1) % NSA_HEAD_DIM
    partner = jnp.where(lane < NSA_HEAD_DIM // 2, pltpu.roll(x, w - NSA_HEAD_DIM // 2, axis=1),
                        pltpu.roll(x, NSA_HEAD_DIM // 2, axis=1))
    return x * cos + partner * sin_signed


def _nsa_prep_body(z_ref, cos_ref, sin_ref, qn_ref, kn_ref, bd_ref, q_ref, ks_ref, kw_ref):
    z = z_ref[...]
    cos = cos_ref[...]
    sin = sin_ref[...]
    bd = bd_ref[...]
    W = NSA_HEADS * NSA_HEAD_DIM
    KW = NSA_GROUPS * NSA_HEAD_DIM

    def head_norm(x, g, n):
        ms = _dot(x * x, bd[:n, :n], HI) * (1.0 / NSA_HEAD_DIM)
        return x * lax.rsqrt(ms + NORM_EPS) * g

    q = head_norm(z[:, 0:W], qn_ref[...], W)
    q_ref[...] = _rope(q, cos, sin) * (NSA_HEAD_DIM ** -0.5)
    ks = head_norm(z[:, W + 2 * KW:W + 3 * KW], kn_ref[0:1, :], KW)
    ks_ref[...] = _rope(ks, cos[:, :KW], sin[:, :KW])
    kw = head_norm(z[:, W + 4 * KW:W + 5 * KW], kn_ref[1:2, :], KW)
    kw_ref[...] = _rope(kw, cos[:, :KW], sin[:, :KW])


def _rope_tables(pos, width):
    inv = 1.0 / (ROPE_THETA ** (jnp.arange(0, NSA_HEAD_DIM, 2, dtype=F32) / NSA_HEAD_DIM))
    ang = pos.astype(F32)[:, None] * inv[None, :]
    c, s = jnp.cos(ang), jnp.sin(ang)
    reps = width // NSA_HEAD_DIM
    return jnp.tile(jnp.concatenate([c, c], -1), (1, reps)), jnp.tile(jnp.concatenate([-s, s], -1), (1, reps))


def nsa_prep(z_rest, p, seq, cos, sin, tm=256):
    T = z_rest.shape[0]
    W = NSA_HEADS * NSA_HEAD_DIM
    KW = NSA_GROUPS * NSA_HEAD_DIM
    hid = np.arange(W) // NSA_HEAD_DIM
    bd = jnp.asarray((hid[:, None] == hid[None, :]).astype(np.float32))
    qn = jnp.tile(p['nsa_qnorm'], NSA_HEADS).reshape(1, W)
    kn = jnp.stack([jnp.tile(p['nsa_ksnorm'], NSA_GROUPS), jnp.tile(p['nsa_kwnorm'], NSA_GROUPS)])
    tps = seq // tm
    tab = pl.BlockSpec((tm, W), lambda i: (i % tps, 0))
    return pl.pallas_call(
        _nsa_prep_body,
        grid=(T // tm,),
        in_specs=[pl.BlockSpec((tm, SEG), lambda i: (i, 2)), tab, tab,
                  _full((1, W)), _full((2, KW)), _full((W, W))],
        out_specs=[pl.BlockSpec((tm, W), lambda i: (i, 0)), pl.BlockSpec((tm, KW), lambda i: (i, 0)),
                   pl.BlockSpec((tm, KW), lambda i: (i, 0))],
        out_shape=[jax.ShapeDtypeStruct((T, W), F32), jax.ShapeDtypeStruct((T, KW), F32),
                   jax.ShapeDtypeStruct((T, KW), F32)],
        compiler_params=_cparams("parallel"),
        name="nsa_prep",
    )(z_rest, cos, sin, qn, kn, bd)


def _nsa_compress_body(xk_ref, xv_ref, pek_ref, pev_ref, w1k_ref, w1v_ref, w2k_ref, w2v_ref, kn_ref,
                       cos_ref, sin_ref, kc_ref, vc_ref):
    def compress(x, pe_ref, w1_ref, w2_ref):
        a = _dot((x + pe_ref[0:1, :]).astype(BF16), w1_ref[0])
        b = _dot((x + pe_ref[1:2, :]).astype(BF16), w1_ref[1])
        n = a.shape[0]
        h = jax.nn.gelu(a + pltpu.roll(b, n - 1, axis=0))
        return _dot(h.astype(BF16), w2_ref[...])

    kc = compress(xk_ref[0, 0], pek_ref, w1k_ref, w2k_ref)
    ms = jnp.mean(kc * kc, axis=-1, keepdims=True)
    kc = kc * lax.rsqrt(ms + NORM_EPS) * kn_ref[...]
    kc_ref[0, 0] = kc * cos_ref[...] + pltpu.roll(kc, NSA_HEAD_DIM // 2, axis=1) * sin_ref[...]
    vc_ref[0, 0] = compress(xv_ref[0, 0], pev_ref, w1v_ref, w2v_ref).astype(BF16)


def nsa_compress(kc_raw, vc_raw, p, batch, seq):
    G, d = NSA_GROUPS, NSA_HEAD_DIM
    n_chunk = seq // CMP_STRIDE
    RW = NSA_REP * d

    def chunks(t):
        t = t.reshape(batch, n_chunk, CMP_STRIDE, G, d).transpose(0, 3, 1, 2, 4)
        return t.reshape(batch, G, n_chunk, CMP_STRIDE * d)

    def w1(t):
        return t.reshape(2, CMP_STRIDE * d, d).astype(BF16)

    def pe(t):
        return t.reshape(2, CMP_STRIDE * d)

    c_end = jnp.arange(n_chunk) * CMP_STRIDE + (CMP_LEN - 1)
    cos_c, sin_c = _rope_tables(c_end, RW)
    blk = pl.BlockSpec((1, 1, n_chunk, CMP_STRIDE * d), lambda b, g: (b, g, 0, 0))
    oblk = pl.BlockSpec((1, 1, n_chunk, RW), lambda b, g: (b, g, 0, 0))
    return pl.pallas_call(
        _nsa_compress_body,
        grid=(batch, G),
        in_specs=[blk, blk, _full((2, CMP_STRIDE * d)), _full((2, CMP_STRIDE * d)),
                  _full((2, CMP_STRIDE * d, d)), _full((2, CMP_STRIDE * d, d)),
                  _full((d, RW)), _full((d, RW)), _full((1, RW)), _full((n_chunk, RW)), _full((n_chunk, RW))],
        out_specs=[oblk, oblk],
        out_shape=[jax.ShapeDtypeStruct((batch, G, n_chunk, RW), F32),
                   jax.ShapeDtypeStruct((batch, G, n_chunk, RW), BF16)],
        compiler_params=_cparams("parallel", "parallel"),
        name="nsa_compress",
    )(chunks(kc_raw), chunks(vc_raw), pe(p['nsa_pe_k']), pe(p['nsa_pe_v']), w1(p['nsa_c1_k']), w1(p['nsa_c1_v']),
      jnp.tile(p['nsa_c2_k'], (1, NSA_REP)).astype(BF16), jnp.tile(p['nsa_c2_v'], (1, NSA_REP)).astype(BF16),
      jnp.tile(p['nsa_kcnorm'], NSA_REP).reshape(1, RW), cos_c, sin_c)


def _masked_softmax(s, valid):
    m = jnp.max(jnp.where(valid, s, NEG), axis=-1, keepdims=True)
    e = jnp.where(valid, jnp.exp(s - m), 0.0)
    return e / jnp.maximum(jnp.sum(e, axis=-1, keepdims=True), jnp.finfo(F32).tiny)


def _nsa_attn_body(q_ref, kc_ref, vc_ref, ks_ref, vs_ref, kw0, kw1, kw2, kw3, kw4, vw0, vw1, vw2, vw3, vw4,
                   zs_ref, ov_ref, e0_ref, eg_ref, o_ref, m_ref, l_ref, acc_ref, *, n_cmp):
    QB = Q_BLOCK
    R = NSA_REP
    RW = R * NSA_HEAD_DIM
    KT = 8 * SEL_LEN
    q0 = pl.program_id(2) * QB
    q = q_ref[0]
    head = lax.broadcasted_iota(jnp.int32, (QB, RW), 1) // NSA_HEAD_DIM
    q4 = jnp.concatenate([jnp.where(head == r, q, 0.0) for r in range(R)], axis=0)
    q4b = q4.astype(BF16)

    def fold(o4):
        out = jnp.where(head == 0, o4[0:QB], 0.0)
        for r in range(1, R):
            out = out + jnp.where(head == r, o4[r * QB:(r + 1) * QB], 0.0)
        return out

    NC = kc_ref.shape[2]
    t_c = q0 + lax.broadcasted_iota(jnp.int32, (R * QB, NC), 0) % QB
    n_c = lax.broadcasted_iota(jnp.int32, (R * QB, NC), 1)
    valid_c = (n_c * CMP_STRIDE + (CMP_LEN - 1) <= t_c) & (n_c < n_cmp)
    p_c = _masked_softmax(_dot_nt(q4, kc_ref[0, 0], HI), valid_c)
    o_c = fold(_dot(p_c.astype(BF16), vc_ref[0, 0]))

    psum = p_c[0:QB]
    for r in range(1, R):
        psum = psum + p_c[r * QB:(r + 1) * QB]
    NB = ov_ref.shape[1]
    imp = _dot(psum, ov_ref[...], HI)
    t_b = q0 + lax.broadcasted_iota(jnp.int32, (QB, NB), 0)
    blk = lax.broadcasted_iota(jnp.int32, (QB, NB), 1)
    cur = t_b // SEL_LEN
    forced = (blk == 0) | (blk == cur) | (blk == cur - 1)
    score = jnp.where(forced, -NEG, jnp.where(blk * SEL_LEN <= t_b, imp, NEG))
    member = jnp.zeros((QB, NB), F32)
    for _ in range(SEL_TOPK):
        mx = jnp.max(score, axis=-1, keepdims=True)
        first = jnp.min(jnp.where(score == mx, blk, NB), axis=-1, keepdims=True)
        pick = blk == first
        member = jnp.where(pick, 1.0, member)
        score = jnp.where(pick, NEG, score)

    m_ref[...] = jnp.full(m_ref.shape, NEG, F32)
    l_ref[...] = jnp.zeros(l_ref.shape, F32)
    acc_ref[...] = jnp.zeros(acc_ref.shape, F32)
    t_s = q0 + lax.broadcasted_iota(jnp.int32, (QB, KT), 0)
    col_s = lax.broadcasted_iota(jnp.int32, (QB, KT), 1)

    def tile(j, carry):
        k0 = pl.multiple_of(j * KT, KT)
        kt = ks_ref[0, 0, pl.ds(k0, KT), :]
        vt = vs_ref[0, 0, pl.ds(k0, KT), :]
        s = _dot_nt(q4b, kt)
        mexp = _dot(pltpu.roll(member, (NB - 8 * j) % NB, axis=1).astype(BF16), e0_ref[...])
        allowed = (mexp > 0.5) & (k0 + col_s <= t_s)
        s = jnp.where(jnp.concatenate([allowed] * R, axis=0), s, NEG)
        m_old = m_ref[...]
        m_new = jnp.maximum(m_old, jnp.max(s, axis=-1, keepdims=True))
        alpha = jnp.exp(m_old - m_new)
        pexp = jnp.exp(s - m_new)
        l_ref[...] = alpha * l_ref[...] + jnp.sum(pexp, axis=-1, keepdims=True)
        acc_ref[...] = alpha * acc_ref[...] + _dot(pexp.astype(BF16), vt)
        m_ref[...] = m_new
        return carry

    lax.fori_loop(0, (q0 + QB + KT - 1) // KT, tile, 0)
    o_s = fold(acc_ref[...] / l_ref[...])

    kwin = jnp.concatenate([kw0[0, 0], kw1[0, 0], kw2[0, 0], kw3[0, 0], kw4[0, 0]], axis=0)
    vwin = jnp.concatenate([vw0[0, 0], vw1[0, 0], vw2[0, 0], vw3[0, 0], vw4[0, 0]], axis=0)
    NW = WINDOW + QB
    t_w = q0 + lax.broadcasted_iota(jnp.int32, (R * QB, NW), 0) % QB
    kpos = q0 - WINDOW + lax.broadcasted_iota(jnp.int32, (R * QB, NW), 1)
    dist = t_w - kpos
    valid_w = (dist >= 0) & (dist < WINDOW) & (kpos >= 0)
    p_w = _masked_softmax(_dot_nt(q4b, kwin), valid_w)
    o_w = fold(_dot(p_w.astype(BF16), vwin))

    gs = jax.nn.sigmoid(zs_ref[0])
    o_ref[0] = (_dot(gs, eg_ref[0, 0], HI) * o_c + _dot(gs, eg_ref[0, 1], HI) * o_s
                + _dot(gs, eg_ref[0, 2], HI) * o_w)


def nsa_attention(q, kc, vc, ks, vs, kw, vw, z_rest3):
    batch, seq, _ = q.shape
    G, R, d = NSA_GROUPS, NSA_REP, NSA_HEAD_DIM
    RW = R * d
    n_chunk = kc.shape[2]
    n_cmp = n_chunk - CMP_LEN // CMP_STRIDE + 1
    n_sel = seq // SEL_LEN
    n = np.arange(n_chunk)[:, None]
    j = np.arange(n_sel)[None, :]
    overlap = ((n * CMP_STRIDE <= j * SEL_LEN + SEL_LEN - 1) & (n * CMP_STRIDE + CMP_LEN - 1 >= j * SEL_LEN)
               & (n < n_cmp)).astype(np.float32)
    e0 = (np.arange(n_sel)[:, None] == (np.arange(8 * SEL_LEN)[None, :] // SEL_LEN)).astype(np.float32)
    eg = np.zeros((G, 3, SMALL, RW), np.float32)
    for g in range(G):
        for r in range(R):
            for c in range(3):
                eg[g, c, S_NSA + 3 * (g * R + r) + c, r * d:(r + 1) * d] = 1.0
    wblk = lambda o: pl.BlockSpec((1, 1, Q_BLOCK, RW), lambda b, g, i: (b, g, i + o, 0))
    return pl.pallas_call(
        functools.partial(_nsa_attn_body, n_cmp=n_cmp),
        grid=(batch, G, seq // Q_BLOCK),
        in_specs=[pl.BlockSpec((1, Q_BLOCK, RW), lambda b, g, i: (b, i, g)),
                  pl.BlockSpec((1, 1, n_chunk, RW), lambda b, g, i: (b, g, 0, 0)),
                  pl.BlockSpec((1, 1, n_chunk, RW), lambda b, g, i: (b, g, 0, 0)),
                  pl.BlockSpec((1, 1, seq, RW), lambda b, g, i: (b, g, 0, 0)),
                  pl.BlockSpec((1, 1, seq, RW), lambda b, g, i: (b, g, 0, 0))]
                 + [wblk(o) for o in range(5)] * 2
                 + [pl.BlockSpec((1, Q_BLOCK, SMALL), lambda b, g, i: (b, i, SMALL_OFF // SMALL)),
                    _full((n_chunk, n_sel)), _full((n_sel, 8 * SEL_LEN)),
                    pl.BlockSpec((1, 3, SMALL, RW), lambda b, g, i: (g, 0, 0, 0))],
        out_specs=pl.BlockSpec((1, Q_BLOCK, RW), lambda b, g, i: (b, i, g)),
        out_shape=jax.ShapeDtypeStruct((batch, seq, G * RW), F32),
        scratch_shapes=[pltpu.VMEM((R * Q_BLOCK, 1), F32), pltpu.VMEM((R * Q_BLOCK, 1), F32),
                        pltpu.VMEM((R * Q_BLOCK, RW), F32)],
        compiler_params=_cparams("parallel", "parallel", "arbitrary"),
        name="nsa_attention",
    )(q, kc, vc, ks, vs, *([kw] * 5), *([vw] * 5), z_rest3, jnp.asarray(overlap), jnp.asarray(e0, BF16),
      jnp.asarray(eg))


def nsa_mix(z_rest3, p, cos, sin):
    batch, seq, _ = z_rest3.shape
    G, R, d = NSA_GROUPS, NSA_REP, NSA_HEAD_DIM
    W = NSA_HEADS * d
    KW = G * d
    z2 = z_rest3.reshape(batch * seq, -1)
    q, ks, kw = nsa_prep(z2, p, seq, cos, sin)
    base = 2 * SEG + W
    raw = lambda i: z_rest3[..., base + i * KW:base + (i + 1) * KW]
    kc, vc = nsa_compress(raw(0), raw(1), p, batch, seq)

    def rep(t, pad):
        t = t.reshape(batch, seq, G, d).transpose(0, 2, 1, 3).astype(BF16)
        t = jnp.tile(t, (1, 1, 1, R))
        return jnp.pad(t, ((0, 0), (0, 0), (pad, 0), (0, 0))) if pad else t

    return nsa_attention(q.reshape(batch, seq, W), kc, vc, rep(ks.reshape(batch, seq, KW), 0), rep(raw(3), 0),
                         rep(kw.reshape(batch, seq, KW), WINDOW), rep(raw(5), WINDOW), z_rest3)


def _merge_body(h_ref, zg_ref, ya_ref, yb_ref, yc_ref, pa_ref, pb_ref, pc_ref, wo_ref, o_ref):
    D = D_MODEL

    def branch(i, y_ref, p_ref):
        gate = jax.nn.sigmoid(zg_ref[:, i * D:(i + 1) * D].astype(F32))
        return gate * _dot(y_ref[...].astype(BF16), p_ref[...])

    merged = branch(0, ya_ref, pa_ref) + branch(1, yb_ref, pb_ref) + branch(2, yc_ref, pc_ref)
    o_ref[...] = h_ref[...] + _dot(merged.astype(BF16), wo_ref[...])


def merge_out(h, z_gate, y_rw, y_gla, y_nsa, p, tm=256):
    T, D = h.shape
    W = y_rw.shape[1]
    row = lambda w: pl.BlockSpec((tm, w), lambda i: (i, 0))
    return pl.pallas_call(
        _merge_body,
        grid=(T // tm,),
        in_specs=[row(D), row(3 * D), row(W), row(W), row(W),
                  _full((W, D)), _full((W, D)), _full((W, D)), _full((D, D))],
        out_specs=row(D),
        out_shape=jax.ShapeDtypeStruct((T, D), F32),
        compiler_params=_cparams("parallel"),
        name="merge_out",
    )(h, z_gate, y_rw, y_gla, y_nsa, p['proj_a'].astype(BF16), p['proj_b'].astype(BF16),
      p['proj_c'].astype(BF16), p['w_out'].astype(BF16))


def _ffn_body(h_ref, g_ref, wg_ref, wu_ref, wd_ref, o_ref, xn_ref, acc_ref):
    f = pl.program_id(1)

    @pl.when(f == 0)
    def _():
        x = h_ref[...]
        ms = jnp.mean(x * x, axis=-1, keepdims=True)
        xn_ref[...] = (x * lax.rsqrt(ms + NORM_EPS) * g_ref[...]).astype(BF16)
        acc_ref[...] = jnp.zeros_like(acc_ref)

    xn = xn_ref[...]
    a = jax.nn.silu(_dot(xn, wg_ref[...])) * _dot(xn, wu_ref[...])
    acc_ref[...] += _dot(a.astype(BF16), wd_ref[...])

    @pl.when(f == pl.num_programs(1) - 1)
    def _():
        o_ref[...] = h_ref[...] + acc_ref[...]


def ffn_dense(h, p, tm=512, tf=512):
    T, D = h.shape
    F = p['ffn_gate'].shape[1]
    return pl.pallas_call(
        _ffn_body,
        grid=(T // tm, F // tf),
        in_specs=[pl.BlockSpec((tm, D), lambda i, f: (i, 0)), _full((1, D)),
                  pl.BlockSpec((D, tf), lambda i, f: (0, f)), pl.BlockSpec((D, tf), lambda i, f: (0, f)),
                  pl.BlockSpec((tf, D), lambda i, f: (f, 0))],
        out_specs=pl.BlockSpec((tm, D), lambda i, f: (i, 0)),
        out_shape=jax.ShapeDtypeStruct((T, D), F32),
        scratch_shapes=[pltpu.VMEM((tm, D), BF16), pltpu.VMEM((tm, D), F32)],
        compiler_params=_cparams("parallel", "arbitrary"),
        name="ffn_dense",
    )(h, p['norm_ffn'].reshape(1, D), p['ffn_gate'].astype(BF16), p['ffn_up'].astype(BF16),
      p['ffn_down'].astype(BF16))


def _router_body(h_ref, g_ref, wr_ref, xn_ref, route_ref):
    x = h_ref[...]
    ms = jnp.mean(x * x, axis=-1, keepdims=True)
    xn = x * lax.rsqrt(ms + NORM_EPS) * g_ref[...]
    xn_ref[...] = xn.astype(BF16)
    logits = _dot(xn, wr_ref[...], HI)
    lane = lax.broadcasted_iota(jnp.int32, logits.shape, 1)
    valid = lane < N_EXPERTS
    probs = _masked_softmax(logits, valid)
    p1 = jnp.max(probs, axis=-1, keepdims=True)
    i1 = jnp.min(jnp.where(valid & (probs == p1), lane, LANES), axis=-1, keepdims=True)
    rest = jnp.where(valid & (lane != i1), probs, -1.0)
    p2 = jnp.max(rest, axis=-1, keepdims=True)
    i2 = jnp.min(jnp.where(rest == p2, lane, LANES), axis=-1, keepdims=True)
    tot = p1 + p2
    route_ref[...] = jnp.where(lane == 0, i1.astype(F32),
                               jnp.where(lane == 1, i2.astype(F32),
                                         jnp.where(lane == 2, p1 / tot, jnp.where(lane == 3, p2 / tot, 0.0))))


def moe_router(h, p, tm=512):
    T, D = h.shape
    wr = jnp.zeros((D, LANES), F32).at[:, :N_EXPERTS].set(p['router'])
    return pl.pallas_call(
        _router_body,
        grid=(T // tm,),
        in_specs=[pl.BlockSpec((tm, D), lambda i: (i, 0)), _full((1, D)), _full((D, LANES))],
        out_specs=[pl.BlockSpec((tm, D), lambda i: (i, 0)), pl.BlockSpec((tm, LANES), lambda i: (i, 0))],
        out_shape=[jax.ShapeDtypeStruct((T, D), BF16), jax.ShapeDtypeStruct((T, LANES), F32)],
        compiler_params=_cparams("parallel"),
        name="moe_router",
    )(h, p['norm_ffn'].reshape(1, D), wr)


def _moe_body(te_ref, tv_ref, x_ref, wg_ref, wu_ref, wd_ref, o_ref, acc_ref):
    t = pl.program_id(0)
    f = pl.program_id(1)

    @pl.when(f == 0)
    def _():
        acc_ref[...] = jnp.zeros_like(acc_ref)

    @pl.when(tv_ref[t] > 0)
    def _():
        x = x_ref[...]
        a = jax.nn.silu(_dot(x, wg_ref[0])) * _dot(x, wu_ref[0])
        acc_ref[...] += _dot(a.astype(BF16), wd_ref[0])

    @pl.when(f == pl.num_programs(1) - 1)
    def _():
        o_ref[...] = acc_ref[...]


def moe_grouped(x_sorted, tile_expert, tile_valid, p, tm, tf=1024):
    P, D = x_sorted.shape
    F = p['exp_gate'].shape[2]
    nf = F // tf

    def fidx(f, tv, t):
        return jnp.where(tv[t] > 0, f, nf - 1)

    grid_spec = pltpu.PrefetchScalarGridSpec(
        num_scalar_prefetch=2,
        grid=(P // tm, nf),
        in_specs=[pl.BlockSpec((tm, D), lambda t, f, te, tv: (t, 0)),
                  pl.BlockSpec((1, D, tf), lambda t, f, te, tv: (te[t], 0, fidx(f, tv, t))),
                  pl.BlockSpec((1, D, tf), lambda t, f, te, tv: (te[t], 0, fidx(f, tv, t))),
                  pl.BlockSpec((1, tf, D), lambda t, f, te, tv: (te[t], fidx(f, tv, t), 0))],
        out_specs=pl.BlockSpec((tm, D), lambda t, f, te, tv: (t, 0)),
        scratch_shapes=[pltpu.VMEM((tm, D), F32)],
    )
    return pl.pallas_call(
        _moe_body,
        grid_spec=grid_spec,
        out_shape=jax.ShapeDtypeStruct((P, D), F32),
        compiler_params=_cparams("arbitrary", "arbitrary"),
        name="moe_grouped",
    )(tile_expert, tile_valid, x_sorted, p['exp_gate'].astype(BF16), p['exp_up'].astype(BF16),
      p['exp_down'].astype(BF16))


def _combine_body(h_ref, ya_ref, yb_ref, route_ref, o_ref):
    w = route_ref[...]
    o_ref[...] = h_ref[...] + w[:, 2:3] * ya_ref[...] + w[:, 3:4] * yb_ref[...]


def moe_combine(h, ya, yb, route, tm=512):
    T, D = h.shape
    row = lambda w: pl.BlockSpec((tm, w), lambda i: (i, 0))
    return pl.pallas_call(
        _combine_body,
        grid=(T // tm,),
        in_specs=[row(D), row(D), row(D), row(LANES)],
        out_specs=row(D),
        out_shape=jax.ShapeDtypeStruct((T, D), F32),
        compiler_params=_cparams("parallel"),
        name="moe_combine",
    )(h, ya, yb, route)


def moe_ffn(h, p, tm=512):
    T, D = h.shape
    xn, route = moe_router(h, p)
    experts = route[:, 0:TOP_K].astype(jnp.int32).reshape(-1)
    n_pairs = T * TOP_K
    n_tiles = n_pairs // tm + N_EXPERTS
    counts = jnp.zeros((N_EXPERTS,), jnp.int32).at[experts].add(1)
    tiles_e = (counts + tm - 1) // tm
    tile_end = jnp.cumsum(tiles_e)
    row_start = (tile_end - tiles_e) * tm
    order = jnp.argsort(experts, stable=True)
    sorted_e = experts[order]
    rank = jnp.arange(n_pairs, dtype=jnp.int32) - (jnp.cumsum(counts) - counts)[sorted_e]
    dest_sorted = row_start[sorted_e] + rank
    dest = jnp.zeros((n_pairs,), jnp.int32).at[order].set(dest_sorted)
    row_token = jnp.zeros((n_tiles * tm,), jnp.int32).at[dest_sorted].set(order // TOP_K)
    tiles = jnp.arange(n_tiles, dtype=jnp.int32)
    tile_valid = (tiles < tile_end[-1]).astype(jnp.int32)
    tile_expert = jnp.minimum(jnp.searchsorted(tile_end, tiles, side='right'), N_EXPERTS - 1).astype(jnp.int32)
    last_e = tile_expert[jnp.maximum(tile_end[-1] - 1, 0)]
    tile_expert = jnp.where(tile_valid > 0, tile_expert, last_e)
    y_sorted = moe_grouped(xn[row_token], tile_expert, tile_valid, p, tm)
    dest = dest.reshape(T, TOP_K)
    return moe_combine(h, y_sorted[dest[:, 0]], y_sorted[dest[:, 1]], route)


def _reorder_w_in(w_in, has_vres):
    D = D_MODEL
    n_rw = 3 * RW_WIDTH + 160 + (32 if has_vres else 0)
    rw0 = 3 * D
    gla0 = rw0 + n_rw
    nsa0 = gla0 + 1552
    col = lambda a, b: w_in[:, a:b]
    zeros = lambda n: jnp.zeros((D, n), w_in.dtype)
    small = [col(rw0 + 1536, rw0 + n_rw)] + ([] if has_vres else [zeros(32)])
    small += [col(gla0 + 1024, gla0 + 1040), col(nsa0 + 1280, nsa0 + 1304), zeros(SMALL - S_NSA - 24)]
    rest = [col(rw0, rw0 + 1536), col(gla0, gla0 + 1024), col(gla0 + 1040, gla0 + 1552),
            col(nsa0, nsa0 + 1280)] + small
    return w_in[:, :rw0].astype(BF16), jnp.concatenate(rest, axis=1).astype(BF16)


def _layer(h, p, v_first, batch, seq, cos, sin, is_moe):
    T = batch * seq
    w_gate, w_rest = _reorder_w_in(p['w_in'], v_first is not None)
    z_gate = norm_matmul(h, p['norm_mix'], w_gate, BF16, 512, 768)
    z_rest = norm_matmul(h, p['norm_mix'], w_rest, F32, 512, 768)
    z_rest3 = z_rest.reshape(batch, seq, -1)
    prep = rwkv_prep(z_rest, v_first, p, seq)
    if v_first is None:
        v_first = prep[3]
    y_rw = rwkv_chunk(prep, p, batch, seq).reshape(T, -1)
    y_gla = gla_mix(z_rest3, p).reshape(T, -1)
    y_nsa = nsa_mix(z_rest3, p, cos, sin).reshape(T, -1)
    h = merge_out(h, z_gate, y_rw, y_gla, y_nsa, p)
    h = moe_ffn(h, p) if is_moe else ffn_dense(h, p)
    return h, v_first


_L0 = ('norm_mix', 'w_in', 'rw_mu', 'rw_w0', 'rw_w2', 'rw_a0', 'rw_a2', 'rw_g2', 'rw_kk', 'rw_ka', 'rw_rk',
       'rw_ln_w', 'rw_ln_b', 'gla_wg', 'gla_bg', 'gla_norm', 'nsa_qnorm', 'nsa_kcnorm', 'nsa_ksnorm',
       'nsa_kwnorm', 'nsa_pe_k', 'nsa_c1_k', 'nsa_c2_k', 'nsa_pe_v', 'nsa_c1_v', 'nsa_c2_v', 'proj_a', 'proj_b',
       'proj_c', 'w_out', 'norm_ffn', 'ffn_gate', 'ffn_up', 'ffn_down')
_L1 = ('norm_mix', 'w_in', 'rw_mu', 'rw_w0', 'rw_w2', 'rw_a0', 'rw_a2', 'rw_g2', 'rw_v0', 'rw_v2', 'rw_kk',
       'rw_ka', 'rw_rk', 'rw_ln_w', 'rw_ln_b', 'gla_wg', 'gla_bg', 'gla_norm', 'nsa_qnorm', 'nsa_kcnorm',
       'nsa_ksnorm', 'nsa_kwnorm', 'nsa_pe_k', 'nsa_c1_k', 'nsa_c2_k', 'nsa_pe_v', 'nsa_c1_v', 'nsa_c2_v',
       'proj_a', 'proj_b', 'proj_c', 'w_out', 'norm_ffn', 'router', 'exp_gate', 'exp_up', 'exp_down')


def kernel(x, *weights):
    p0 = dict(zip(_L0, weights[:len(_L0)]))
    p1 = dict(zip(_L1, weights[len(_L0):]))
    batch, seq, D = x.shape
    cos, sin = _rope_tables(jnp.arange(seq), NSA_HEADS * NSA_HEAD_DIM)
    h = x.reshape(batch * seq, D)
    h, v_first = _layer(h, p0, None, batch, seq, cos, sin, False)
    h, _ = _layer(h, p1, v_first, batch, seq, cos, sin, True)
    return h.reshape(batch, seq, D)
```

```python
import functools

import numpy as np
import jax
import jax.numpy as jnp
from jax import lax
from jax.experimental import pallas as pl
from jax.experimental.pallas import tpu as pltpu

F32 = jnp.float32
BF16 = jnp.bfloat16
HI = lax.Precision.HIGHEST

D_MODEL = 2048
RW_HEADS = 8
RW_HEAD_DIM = 64
RW_WIDTH = 512
RW_GN_EPS = 64e-5
GLA_HEADS = 4
GLA_DK = 64
GLA_DV = 128
GLA_TAU = 16.0
CHUNK = 64
NSA_HEADS = 8
NSA_GROUPS = 2
NSA_REP = NSA_HEADS // NSA_GROUPS
NSA_HEAD_DIM = 64
KV_REP = 2
SEL_TILE_BLOCKS = 16
CMP_LEN = 32
CMP_STRIDE = 16
SEL_LEN = 64
SEL_TOPK = 16
WINDOW = 512
Q_BLOCK = 128
ROPE_THETA = 10000.0
N_EXPERTS = 8
TOP_K = 2
NORM_EPS = 1e-6

LANES = 128
VMEM_LIMIT = 56 * 1024 * 1024

SEG = 1536
SMALL = 256
SMALL_OFF = 3 * SEG - SMALL
S_ZW, S_ZA, S_ZG, S_ZV, S_GLA, S_NSA = 0, 32, 64, 160, 192, 208
NEG = -1e30
LOG2E = 1.4426950408889634


def _dot(a, b, precision=None):
    return jnp.dot(a, b, preferred_element_type=F32, precision=precision)


def _dot_nt(a, b, precision=None):
    return lax.dot_general(a, b, (((1,), (1,)), ((), ())), preferred_element_type=F32, precision=precision)


def _dot_tn(a, b, precision=None):
    return lax.dot_general(a, b, (((0,), (0,)), ((), ())), preferred_element_type=F32, precision=precision)


def _cparams(*sem):
    return pltpu.CompilerParams(dimension_semantics=sem, vmem_limit_bytes=VMEM_LIMIT)


def _full(shape):
    n = len(shape)
    return pl.BlockSpec(shape, lambda *_: (0,) * n)


def _norm_matmul_body(x_ref, g_ref, w_ref, o_ref, xn_ref):
    @pl.when(pl.program_id(1) == 0)
    def _():
        x = x_ref[...]
        ms = jnp.mean(x * x, axis=-1, keepdims=True)
        xn_ref[...] = (x * lax.rsqrt(ms + NORM_EPS) * g_ref[...]).astype(BF16)

    o_ref[...] = _dot(xn_ref[...], w_ref[...]).astype(o_ref.dtype)


def norm_matmul(x, g, w, out_dtype, tm, tn):
    M, K = x.shape
    N = w.shape[1]
    return pl.pallas_call(
        _norm_matmul_body,
        grid=(M // tm, N // tn),
        in_specs=[pl.BlockSpec((tm, K), lambda i, j: (i, 0)),
                  pl.BlockSpec((1, K), lambda i, j: (0, 0)),
                  pl.BlockSpec((K, tn), lambda i, j: (0, j))],
        out_specs=pl.BlockSpec((tm, tn), lambda i, j: (i, j)),
        out_shape=jax.ShapeDtypeStruct((M, N), out_dtype),
        scratch_shapes=[pltpu.VMEM((tm, K), BF16)],
        compiler_params=_cparams("parallel", "arbitrary"),
        name="norm_matmul",
    )(x, g.reshape(1, K), w)


def _shift_rows(x, prev_row):
    rolled = pltpu.roll(x, 1, axis=0)
    first = lax.broadcasted_iota(jnp.int32, x.shape, 0) == 0
    return jnp.where(first, prev_row, rolled)


def _rwkv_prep_body(*refs, tiles_per_seq, has_vres):
    if has_vres:
        (zm_ref, zs_ref, vf_ref, mum_ref, mus_ref, vec_ref, w2_ref, a2_ref, g2_ref, v2_ref, bd_ref,
         r_ref, lw_ref, k_ref, v_ref, kk_ref, b_ref, g_ref, bonus_ref, pm_ref, ps_ref) = refs
    else:
        (zm_ref, zs_ref, mum_ref, mus_ref, vec_ref, w2_ref, a2_ref, g2_ref, bd_ref,
         r_ref, lw_ref, k_ref, v_ref, kk_ref, b_ref, g_ref, bonus_ref, pm_ref, ps_ref) = refs
    i = pl.program_id(0)

    @pl.when(i % tiles_per_seq == 0)
    def _():
        pm_ref[...] = jnp.zeros_like(pm_ref)
        ps_ref[...] = jnp.zeros_like(ps_ref)

    zm = zm_ref[...]
    zs = zs_ref[...]
    tm = zm.shape[0]
    zm_prev = _shift_rows(zm, pm_ref[0:1, :])
    zs_prev = _shift_rows(zs, ps_ref[0:1, :])
    pm_ref[0:1, :] = zm[tm - 1:tm, :]
    ps_ref[0:1, :] = zs[tm - 1:tm, :]
    zm = zm + (zm_prev - zm) * mum_ref[...]
    zs = zs + (zs_prev - zs) * mus_ref[...]

    r = zm[:, 0:RW_WIDTH]
    k = zm[:, RW_WIDTH:2 * RW_WIDTH]
    v = zm[:, 2 * RW_WIDTH:3 * RW_WIDTH]
    w0 = vec_ref[0:1, :]
    a0 = vec_ref[1:2, :]
    v0 = vec_ref[2:3, :]
    kkw = vec_ref[3:4, :]
    ka = vec_ref[4:5, :]
    rk = vec_ref[5:6, :]

    w_log = -jax.nn.softplus(-(w0 + _dot(jnp.tanh(zs), w2_ref[...], HI))) - 0.5
    lw = -jnp.exp(w_log)
    a = jax.nn.sigmoid(a0 + _dot(zs, a2_ref[...], HI))
    g = _dot(jax.nn.sigmoid(zs), g2_ref[...], HI)
    if has_vres:
        v = v + (vf_ref[...] - v) * jax.nn.sigmoid(v0 + _dot(zs, v2_ref[...], HI))

    bd = bd_ref[...]
    kk = k * kkw
    nrm = jnp.sqrt(_dot(kk * kk, bd, HI))
    kk = kk / jnp.maximum(nrm, 1e-12)
    k2 = k * (1.0 + (a - 1.0) * ka)
    bonus = _dot(r * k2 * rk, bd, HI) * v

    r_ref[...] = r
    lw_ref[...] = lw
    k_ref[...] = k2
    v_ref[...] = v
    kk_ref[...] = kk
    b_ref[...] = kk * a
    g_ref[...] = g
    bonus_ref[...] = bonus


def _pad_rows(w, off, total=SMALL):
    return jnp.zeros((total, w.shape[1]), F32).at[off:off + w.shape[0]].set(w)


def rwkv_prep(z_rest, v_first, p, seq, tm=256):
    T = z_rest.shape[0]
    has_vres = v_first is not None
    mu = p['rw_mu']
    n_small = mu.shape[0] - 3 * RW_WIDTH
    mum = mu[:3 * RW_WIDTH].reshape(1, -1)
    mus = jnp.zeros((SMALL,), F32).at[:n_small].set(mu[3 * RW_WIDTH:]).reshape(1, -1)
    zero = jnp.zeros((RW_WIDTH,), F32)
    vec = jnp.stack([p['rw_w0'], p['rw_a0'], p['rw_v0'] if has_vres else zero, p['rw_kk'], p['rw_ka'],
                     p['rw_rk'].reshape(-1), zero, zero])
    w2 = _pad_rows(p['rw_w2'], S_ZW)
    a2 = _pad_rows(p['rw_a2'], S_ZA)
    g2 = _pad_rows(p['rw_g2'], S_ZG)
    hid = np.arange(RW_WIDTH) // RW_HEAD_DIM
    bd = jnp.asarray((hid[:, None] == hid[None, :]).astype(np.float32))
    row = lambda w: pl.BlockSpec((tm, w), lambda i: (i, 0))
    in_specs = [row(SEG), pl.BlockSpec((tm, SMALL), lambda i: (i, SMALL_OFF // SMALL))]
    args = [z_rest, z_rest]
    if has_vres:
        in_specs.append(row(RW_WIDTH))
        args.append(v_first)
    in_specs += [_full((1, SEG)), _full((1, SMALL)), _full((8, RW_WIDTH)),
                 _full((SMALL, RW_WIDTH)), _full((SMALL, RW_WIDTH)), _full((SMALL, RW_WIDTH))]
    args += [mum, mus, vec, w2, a2, g2]
    if has_vres:
        in_specs.append(_full((SMALL, RW_WIDTH)))
        args.append(_pad_rows(p['rw_v2'], S_ZV))
    in_specs.append(_full((RW_WIDTH, RW_WIDTH)))
    args.append(bd)
    out = jax.ShapeDtypeStruct((T, RW_WIDTH), F32)
    return pl.pallas_call(
        functools.partial(_rwkv_prep_body, tiles_per_seq=seq // tm, has_vres=has_vres),
        grid=(T // tm,),
        in_specs=in_specs,
        out_specs=[row(RW_WIDTH)] * 8,
        out_shape=[out] * 8,
        scratch_shapes=[pltpu.VMEM((8, SEG), F32), pltpu.VMEM((8, SMALL), F32)],
        compiler_params=_cparams("arbitrary"),
        name="rwkv_prep",
    )(*args)


def _unit_lower_inverse(L, eye, masks):
    T = eye
    for mk in masks:
        lo = jnp.where(mk, L, 0.0).astype(BF16)
        n = _dot(T.astype(BF16), lo)
        T = T - _dot(n.astype(BF16), T.astype(BF16))
    return T


def _rwkv_chunk_body(r_ref, lw_ref, k_ref, v_ref, kk_ref, b_ref, g_ref, bonus_ref, lnw_ref, lnb_ref,
                     o_ref, state_ref, *, n_chunks):
    C = CHUNK
    N = RW_HEAD_DIM

    @pl.when(pl.program_id(1) == 0)
    def _():
        state_ref[...] = jnp.zeros_like(state_ref)

    ti = lax.broadcasted_iota(jnp.int32, (C, C), 0)
    si = lax.broadcasted_iota(jnp.int32, (C, C), 1)
    incl = si <= ti
    strict = si < ti
    tril1 = incl.astype(F32)
    eye = (si == ti).astype(F32)
    masks = []
    m = 1
    while m < C:
        masks.append(((ti // (2 * m)) == (si // (2 * m))) & ((ti // m) % 2 == 1) & ((si // m) % 2 == 0))
        m *= 2
    lnw = lnw_ref[...]
    lnb = lnb_ref[...]

    NCH = range(n_chunks)
    H = range(RW_HEADS)
    U = [(c, h) for c in NCH for h in H]
    sl = [slice(h * N, (h + 1) * N) for h in H]
    ew = []
    for c in NCH:
        rows = pl.ds(c * C, C)
        lw = lw_ref[0, rows, :]
        k = k_ref[0, rows, :]
        b = b_ref[0, rows, :]
        cum = _dot(tril1, lw, HI)
        cum_last = cum[C - 1:C, :]
        e_neg = jnp.exp(-cum)
        e_rel = jnp.exp(cum_last - cum)
        ew.append(dict(
            rows=rows,
            alpha=(kk_ref[0, rows, :] * jnp.exp(cum - lw)).astype(BF16),
            beta=(b * e_neg).astype(BF16), kappa=(k * e_neg).astype(BF16),
            rho=(r_ref[0, rows, :] * jnp.exp(cum)).astype(BF16),
            beta2=(b * e_rel).astype(BF16), kappa2=(k * e_rel).astype(BF16),
            pc=jnp.exp(cum_last), v=v_ref[0, rows, :].astype(BF16)))
    A = [ew[c]['alpha'][:, sl[h]] for c, h in U]
    Rh = [ew[c]['rho'][:, sl[h]] for c, h in U]
    V = [ew[c]['v'][:, sl[h]] for c, h in U]
    p4 = [_dot_nt(jnp.concatenate([A[u], Rh[u]], axis=0),
                  jnp.concatenate([ew[c]['beta'][:, sl[h]], ew[c]['kappa'][:, sl[h]]], axis=0))
          for u, (c, h) in enumerate(U)]
    L = [jnp.where(strict, p[:C, :C], 0.0) for p in p4]
    T = [eye for _ in U]
    for mk in masks:
        n = [_dot(T[u].astype(BF16), jnp.where(mk, L[u], 0.0).astype(BF16)) for u in range(len(U))]
        T = [T[u] - _dot(n[u].astype(BF16), T[u].astype(BF16)) for u in range(len(U))]
    T = [t.astype(BF16) for t in T]
    kv = [_dot(jnp.concatenate([jnp.where(strict, p[:C, C:], 0.0), jnp.where(incl, p[C:, C:], 0.0)],
                               axis=0).astype(BF16), V[u]) for u, p in enumerate(p4)]
    lhs = [jnp.concatenate([_dot(T[u], A[u]).astype(BF16), Rh[u]], axis=0) for u in range(len(U))]
    W = [_dot(T[u], kv[u][:C].astype(BF16)) for u in range(len(U))]
    D = [_dot_tn(V[u], ew[c]['kappa2'][:, sl[h]]) for u, (c, h) in enumerate(U)]
    G = [jnp.where(incl, p[C:, :C], 0.0).astype(BF16) for p in p4]

    S = [state_ref[h] for h in H]
    for c in NCH:
        u0 = c * RW_HEADS
        rs = [_dot_nt(lhs[u0 + h], S[h].astype(BF16)) for h in H]
        Ub = [(-(rs[h][:C] + W[u0 + h])).astype(BF16) for h in H]
        Y = [rs[h][C:] + _dot(G[u0 + h], Ub[h]) + kv[u0 + h][C:] for h in H]
        S = [S[h] * ew[c]['pc'][:, sl[h]] + _dot_tn(Ub[h], ew[c]['beta2'][:, sl[h]]) + D[u0 + h] for h in H]
        yn = []
        for h in H:
            mu = jnp.mean(Y[h], axis=-1, keepdims=True)
            yc = Y[h] - mu
            var = jnp.mean(yc * yc, axis=-1, keepdims=True)
            yn.append(yc * lax.rsqrt(var + RW_GN_EPS))
        rows = ew[c]['rows']
        o_ref[0, rows, :] = ((jnp.concatenate(yn, axis=1) * lnw + lnb + bonus_ref[0, rows, :])
                             * g_ref[0, rows, :])
    state_ref[...] = jnp.stack(S)


def rwkv_chunk(prep, p, batch, seq, rows=256):
    r, lw, k2, v, kk, b, g, bonus = [t.reshape(batch, seq, RW_WIDTH) for t in prep]
    blk = pl.BlockSpec((1, rows, RW_WIDTH), lambda bi, i: (bi, i, 0))
    return pl.pallas_call(
        functools.partial(_rwkv_chunk_body, n_chunks=rows // CHUNK),
        grid=(batch, seq // rows),
        in_specs=[blk] * 8 + [_full((1, RW_WIDTH))] * 2,
        out_specs=blk,
        out_shape=jax.ShapeDtypeStruct((batch, seq, RW_WIDTH), F32),
        scratch_shapes=[pltpu.VMEM((RW_HEADS, RW_HEAD_DIM, RW_HEAD_DIM), F32)],
        compiler_params=_cparams("parallel", "arbitrary"),
        name="rwkv_chunk",
    )(r, lw, k2, v, kk, b, g, bonus, p['rw_ln_w'].reshape(1, -1), p['rw_ln_b'].reshape(1, -1))


def _gla_body(z_ref, zs_ref, wg_ref, bg_ref, gn_ref, o_ref, state_ref, *, n_chunks):
    C = CHUNK
    HK = GLA_HEADS * GLA_DK
    HV = GLA_HEADS * GLA_DV

    @pl.when(pl.program_id(1) == 0)
    def _():
        state_ref[...] = jnp.zeros_like(state_ref)

    ti = lax.broadcasted_iota(jnp.int32, (C, C), 0)
    si = lax.broadcasted_iota(jnp.int32, (C, C), 1)
    tril1 = (si <= ti).astype(F32)
    causal = (lax.broadcasted_iota(jnp.int32, (C, HK), 1) % GLA_DK) <= lax.broadcasted_iota(jnp.int32, (C, HK), 0)
    blk_k = (lax.broadcasted_iota(jnp.int32, (HK, HK), 0) // GLA_DK) == (lax.broadcasted_iota(jnp.int32, (HK, HK), 1) // GLA_DK)
    blk_v = (lax.broadcasted_iota(jnp.int32, (HK, HV), 0) // GLA_DK) == (lax.broadcasted_iota(jnp.int32, (HK, HV), 1) // GLA_DV)
    blk_vt = (lax.broadcasted_iota(jnp.int32, (HV, HK), 0) // GLA_DV) == (lax.broadcasted_iota(jnp.int32, (HV, HK), 1) // GLA_DK)

    def chunk(c, carry):
        rows = pl.ds(pl.multiple_of(c * C, C), C)
        z = z_ref[0, rows, :]
        zs = zs_ref[0, rows, :]
        q = z[:, 0:HK] * (GLA_DK ** -0.5)
        k = z[:, HK:2 * HK]
        v = z[:, 2 * HK:2 * HK + HV]
        zo = z[:, 2 * HK + HV:2 * HK + 2 * HV]
        log_a = jax.nn.log_sigmoid(_dot(zs, wg_ref[...], HI) + bg_ref[...]) / GLA_TAU
        bcum = _dot(tril1, log_a, HI)
        b_last = bcum[C - 1:C, :]
        q_dec = (q * jnp.exp(bcum)).astype(BF16)
        k_inv = (k * jnp.exp(-bcum)).astype(BF16)
        k_dec = (k * jnp.exp(b_last - bcum)).astype(BF16)
        vb = v.astype(BF16)
        kbd = jnp.where(blk_k, jnp.concatenate([k_inv] * GLA_HEADS, axis=0), jnp.zeros((), BF16))
        att = jnp.where(causal, _dot_nt(q_dec, kbd), 0.0)
        vbd = jnp.where(blk_v, jnp.concatenate([vb] * GLA_HEADS, axis=0), jnp.zeros((), BF16))
        st = state_ref[...]
        o = _dot(att.astype(BF16), vbd) + _dot_nt(q_dec, st.astype(BF16))
        state_ref[...] = st * jnp.exp(b_last) + jnp.where(blk_vt, _dot_tn(vb, k_dec), 0.0)
        parts = []
        for h in range(GLA_HEADS):
            oh = o[:, h * GLA_DV:(h + 1) * GLA_DV]
            ms = jnp.mean(oh * oh, axis=-1, keepdims=True)
            parts.append(oh * lax.rsqrt(ms + NORM_EPS))
        on = jnp.concatenate(parts, axis=-1) * gn_ref[...]
        o_ref[0, rows, :] = on * jax.nn.silu(zo)
        return carry

    lax.fori_loop(0, n_chunks, chunk, 0)


def gla_mix(z_rest3, p, rows=256):
    batch, seq, _ = z_rest3.shape
    HV = GLA_HEADS * GLA_DV
    wg = jnp.zeros((SMALL, GLA_HEADS * GLA_DK), F32).at[S_GLA:S_GLA + p['gla_wg'].shape[0]].set(p['gla_wg'])
    return pl.pallas_call(
        functools.partial(_gla_body, n_chunks=rows // CHUNK),
        grid=(batch, seq // rows),
        in_specs=[pl.BlockSpec((1, rows, SEG), lambda b, i: (b, i, 1)),
                  pl.BlockSpec((1, rows, SMALL), lambda b, i: (b, i, SMALL_OFF // SMALL)),
                  _full((SMALL, GLA_HEADS * GLA_DK)), _full((1, GLA_HEADS * GLA_DK)), _full((1, HV))],
        out_specs=pl.BlockSpec((1, rows, HV), lambda b, i: (b, i, 0)),
        out_shape=jax.ShapeDtypeStruct((batch, seq, HV), F32),
        scratch_shapes=[pltpu.VMEM((HV, GLA_HEADS * GLA_DK), F32)],
        compiler_params=_cparams("parallel", "arbitrary"),
        name="gla_mix",
    )(z_rest3, z_rest3, wg, p['gla_bg'].reshape(1, -1), jnp.tile(p['gla_norm'], GLA_HEADS).reshape(1, -1))


def _rope(x, cos, sin_signed):
    w = x.shape[-1]
    lane = lax.broadcasted_iota(jnp.int32, x.shape, 1) % NSA_HEAD_DIM
    partner = jnp.where(lane < NSA_HEAD_DIM // 2, pltpu.roll(x, w - NSA_HEAD_DIM // 2, axis=1),
                        pltpu.roll(x, NSA_HEAD_DIM // 2, axis=1))
    return x * cos + partner * sin_signed


def _split_bf16(x):
    hi = x.astype(BF16)
    return hi, (x - hi.astype(F32)).astype(BF16)


def _nsa_prep_body(z_ref, cos_ref, sin_ref, qn_ref, kn_ref, bd_ref, eg_ref, qh_ref, ql_ref, ks_ref, kw_ref, g_ref):
    z = z_ref[...]
    cos = cos_ref[...]
    sin = sin_ref[...]
    bd = bd_ref[...]
    W = NSA_HEADS * NSA_HEAD_DIM
    KW = NSA_GROUPS * NSA_HEAD_DIM

    def head_norm(x, g, n):
        ms = _dot(x * x, bd[:n, :n], HI) * (1.0 / NSA_HEAD_DIM)
        return x * lax.rsqrt(ms + NORM_EPS) * g

    q = head_norm(z[:, 0:W], qn_ref[...], W)
    qh_ref[...], ql_ref[...] = _split_bf16(_rope(q, cos, sin) * (NSA_HEAD_DIM ** -0.5 * LOG2E))
    ks = head_norm(z[:, W + 2 * KW:W + 3 * KW], kn_ref[0:1, :], KW)
    ks_ref[...] = _rope(ks, cos[:, :KW], sin[:, :KW])
    kw = head_norm(z[:, W + 4 * KW:W + 5 * KW], kn_ref[1:2, :], KW)
    kw_ref[...] = _rope(kw, cos[:, :KW], sin[:, :KW])
    gh, gl = _split_bf16(jax.nn.sigmoid(z[:, SEG - SMALL:SEG]))
    g_ref[...] = _dot(gh, eg_ref[...]) + _dot(gl, eg_ref[...])


def _rope_tables(pos, width):
    inv = 1.0 / (ROPE_THETA ** (jnp.arange(0, NSA_HEAD_DIM, 2, dtype=F32) / NSA_HEAD_DIM))
    ang = pos.astype(F32)[:, None] * inv[None, :]
    c, s = jnp.cos(ang), jnp.sin(ang)
    reps = width // NSA_HEAD_DIM
    return jnp.tile(jnp.concatenate([c, c], -1), (1, reps)), jnp.tile(jnp.concatenate([-s, s], -1), (1, reps))


def nsa_prep(z_rest, p, seq, cos, sin, tm=256):
    T = z_rest.shape[0]
    W = NSA_HEADS * NSA_HEAD_DIM
    KW = NSA_GROUPS * NSA_HEAD_DIM
    hid = np.arange(W) // NSA_HEAD_DIM
    bd = jnp.asarray((hid[:, None] == hid[None, :]).astype(np.float32))
    qn = jnp.tile(p['nsa_qnorm'], NSA_HEADS).reshape(1, W)
    kn = jnp.stack([jnp.tile(p['nsa_ksnorm'], NSA_GROUPS), jnp.tile(p['nsa_kwnorm'], NSA_GROUPS)])
    eg = np.zeros((SMALL, 3 * W), np.float32)
    for h in range(NSA_HEADS):
        for c in range(3):
            eg[S_NSA + 3 * h + c, c * W + h * NSA_HEAD_DIM:c * W + (h + 1) * NSA_HEAD_DIM] = 1.0
    tps = seq // tm
    tab = pl.BlockSpec((tm, W), lambda i: (i % tps, 0))
    row = lambda w: pl.BlockSpec((tm, w), lambda i: (i, 0))
    return pl.pallas_call(
        _nsa_prep_body,
        grid=(T // tm,),
        in_specs=[pl.BlockSpec((tm, SEG), lambda i: (i, 2)), tab, tab,
                  _full((1, W)), _full((2, KW)), _full((W, W)), _full((SMALL, 3 * W))],
        out_specs=[row(W), row(W), row(KW), row(KW), row(3 * W)],
        out_shape=[jax.ShapeDtypeStruct((T, W), BF16), jax.ShapeDtypeStruct((T, W), BF16),
                   jax.ShapeDtypeStruct((T, KW), F32), jax.ShapeDtypeStruct((T, KW), F32),
                   jax.ShapeDtypeStruct((T, 3 * W), F32)],
        compiler_params=_cparams("parallel"),
        name="nsa_prep",
    )(z_rest, cos, sin, qn, kn, bd, jnp.asarray(eg, BF16))


def _nsa_compress_body(xk_ref, xv_ref, pek_ref, pev_ref, w1k_ref, w1v_ref, w2k_ref, w2v_ref, kn_ref,
                       cos_ref, sin_ref, kch_ref, kcl_ref, vc_ref):
    def compress(x, pe_ref, w1_ref, w2_ref):
        a = _dot((x + pe_ref[0:1, :]).astype(BF16), w1_ref[0])
        b = _dot((x + pe_ref[1:2, :]).astype(BF16), w1_ref[1])
        n = a.shape[0]
        h = jax.nn.gelu(a + pltpu.roll(b, n - 1, axis=0))
        return _dot(h.astype(BF16), w2_ref[...])

    kc = compress(xk_ref[0, 0], pek_ref, w1k_ref, w2k_ref)
    ms = jnp.mean(kc * kc, axis=-1, keepdims=True)
    kc = kc * lax.rsqrt(ms + NORM_EPS) * kn_ref[...]
    kc = kc * cos_ref[...] + pltpu.roll(kc, NSA_HEAD_DIM // 2, axis=1) * sin_ref[...]
    kch_ref[0, 0], kcl_ref[0, 0] = _split_bf16(kc)
    vc_ref[0, 0] = compress(xv_ref[0, 0], pev_ref, w1v_ref, w2v_ref).astype(BF16)


def nsa_compress(kc_raw, vc_raw, p, batch, seq):
    G, d = NSA_GROUPS, NSA_HEAD_DIM
    n_chunk = seq // CMP_STRIDE
    RW = KV_REP * d

    def chunks(t):
        t = t.reshape(batch, n_chunk, CMP_STRIDE, G, d).transpose(0, 3, 1, 2, 4)
        return t.reshape(batch, G, n_chunk, CMP_STRIDE * d)

    def w1(t):
        return t.reshape(2, CMP_STRIDE * d, d).astype(BF16)

    def pe(t):
        return t.reshape(2, CMP_STRIDE * d)

    c_end = jnp.arange(n_chunk) * CMP_STRIDE + (CMP_LEN - 1)
    cos_c, sin_c = _rope_tables(c_end, RW)
    blk = pl.BlockSpec((1, 1, n_chunk, CMP_STRIDE * d), lambda b, g: (b, g, 0, 0))
    oblk = pl.BlockSpec((1, 1, n_chunk, RW), lambda b, g: (b, g, 0, 0))
    return pl.pallas_call(
        _nsa_compress_body,
        grid=(batch, G),
        in_specs=[blk, blk, _full((2, CMP_STRIDE * d)), _full((2, CMP_STRIDE * d)),
                  _full((2, CMP_STRIDE * d, d)), _full((2, CMP_STRIDE * d, d)),
                  _full((d, RW)), _full((d, RW)), _full((1, RW)), _full((n_chunk, RW)), _full((n_chunk, RW))],
        out_specs=[oblk, oblk, oblk],
        out_shape=[jax.ShapeDtypeStruct((batch, G, n_chunk, RW), BF16)] * 3,
        compiler_params=_cparams("parallel", "parallel"),
        name="nsa_compress",
    )(chunks(kc_raw), chunks(vc_raw), pe(p['nsa_pe_k']), pe(p['nsa_pe_v']), w1(p['nsa_c1_k']), w1(p['nsa_c1_v']),
      jnp.tile(p['nsa_c2_k'], (1, KV_REP)).astype(BF16), jnp.tile(p['nsa_c2_v'], (1, KV_REP)).astype(BF16),
      jnp.tile(p['nsa_kcnorm'], KV_REP).reshape(1, RW), cos_c, sin_c)


def _masked_softmax(s, valid):
    m = jnp.max(jnp.where(valid, s, NEG), axis=-1, keepdims=True)
    e = jnp.where(valid, jnp.exp(s - m), 0.0)
    return e / jnp.maximum(jnp.sum(e, axis=-1, keepdims=True), jnp.finfo(F32).tiny)


def _nsa_attn_body(qh_ref, ql_ref, kch_ref, kcl_ref, vc_ref, ks_ref, vst_ref,
                   kw0, kw1, kw2, kw3, kw4, vw0, vw1, vw2, vw3, vw4,
                   g0_ref, g1_ref, g2_ref, ov_ref, e0_ref, bw_ref, o_ref, m_ref, l_ref, acc_ref, *, n_cmp):
    QB = Q_BLOCK
    R = NSA_REP
    KT = SEL_TILE_BLOCKS * SEL_LEN
    q0 = pl.program_id(2) * QB
    tiny = jnp.finfo(F32).tiny
    half = lax.broadcasted_iota(jnp.int32, (QB, LANES), 1) // NSA_HEAD_DIM

    def stack(q):
        return jnp.concatenate(
            [jnp.where(half == r % 2, q[:, (r // 2) * LANES:(r // 2 + 1) * LANES], jnp.zeros((), q.dtype))
             for r in range(R)], axis=0)

    def fold(o4):
        return jnp.concatenate([jnp.where(half == 0, o4[0:QB], o4[QB:2 * QB]),
                                jnp.where(half == 0, o4[2 * QB:3 * QB], o4[3 * QB:4 * QB])], axis=1)

    def rows4(x):
        return jnp.concatenate([x] * R, axis=0)

    q4h = stack(qh_ref[0])
    q4l = stack(ql_ref[0])

    NC = kch_ref.shape[2]
    kch = kch_ref[0, 0]
    t_c = q0 + lax.broadcasted_iota(jnp.int32, (QB, NC), 0)
    n_c = lax.broadcasted_iota(jnp.int32, (QB, NC), 1)
    bias_c = jnp.where(n_c * CMP_STRIDE + (CMP_LEN - 1) <= t_c, jnp.where(n_c < n_cmp, 0.0, NEG), NEG)
    s_c = _dot_nt(q4h, kch) + _dot_nt(q4h, kcl_ref[0, 0]) + _dot_nt(q4l, kch) + rows4(bias_c)
    m_c = jnp.maximum(jnp.max(s_c, axis=-1, keepdims=True), 0.1 * NEG)
    e_c = jnp.exp2(s_c - m_c)
    r_c = 1.0 / jnp.maximum(jnp.sum(e_c, axis=-1, keepdims=True), tiny)
    e_cb = e_c.astype(BF16)
    o_c = fold(_dot(e_cb, vc_ref[0, 0]) * r_c)

    NB = ov_ref.shape[1]
    imp = _dot(e_cb[0:QB], ov_ref[...]) * r_c[0:QB]
    for r in range(1, R):
        imp = imp + _dot(e_cb[r * QB:(r + 1) * QB], ov_ref[...]) * r_c[r * QB:(r + 1) * QB]
    t_b = q0 + lax.broadcasted_iota(jnp.int32, (QB, NB), 0)
    blk_i = lax.broadcasted_iota(jnp.int32, (QB, NB), 1)
    blk = blk_i.astype(F32)
    cur = t_b // SEL_LEN
    score = jnp.where(blk_i * SEL_LEN <= t_b, imp, NEG)
    score = jnp.where(blk_i == 0, -NEG, jnp.where(blk_i == cur, -NEG, jnp.where(blk_i == cur - 1, -NEG, score)))
    mbias = jnp.full((QB, NB), NEG, F32)
    for _ in range(min(SEL_TOPK, NB)):
        mx = jnp.max(score, axis=-1, keepdims=True)
        first = jnp.min(jnp.where(score == mx, blk, float(NB)), axis=-1, keepdims=True)
        pick = blk == first
        mbias = jnp.where(pick, 0.0, mbias)
        score = jnp.where(pick, NEG, score)

    m_ref[...] = jnp.full(m_ref.shape, NEG, F32)
    l_ref[...] = jnp.zeros(l_ref.shape, F32)
    acc_ref[...] = jnp.zeros(acc_ref.shape, F32)
    e0 = e0_ref[...]

    def tile(j, diagonal):
        k0 = pl.multiple_of(j * KT, KT)
        kt = ks_ref[0, 0, pl.ds(k0, KT), :]
        vt = vst_ref[0, 0, :, pl.ds(k0, KT)]
        mb = pltpu.roll(mbias, (NB - SEL_TILE_BLOCKS * j) % NB, axis=1)
        if NB < LANES:
            mb = jnp.concatenate([mb, jnp.zeros((QB, LANES - NB), F32)], axis=1)
        mb = mb[:, :LANES].astype(BF16)
        s = _dot_nt(jnp.concatenate([kt, e0], axis=1), jnp.concatenate([q4h, rows4(mb)], axis=1))
        if diagonal:
            key = k0 + lax.broadcasted_iota(jnp.int32, (KT, QB), 0)
            t_s = q0 + lax.broadcasted_iota(jnp.int32, (KT, QB), 1)
            s = s + jnp.concatenate([jnp.where(key <= t_s, 0.0, NEG)] * R, axis=1)
        m_old = m_ref[...]
        m_new = jnp.maximum(m_old, jnp.max(s, axis=0, keepdims=True))
        alpha = jnp.exp2(m_old - m_new)
        pexp = jnp.exp2(s - m_new)
        l_ref[...] = alpha * l_ref[...] + jnp.sum(pexp, axis=0, keepdims=True)
        acc_ref[...] = alpha * acc_ref[...] + _dot(vt, pexp.astype(BF16))
        m_ref[...] = m_new

    n_full = q0 // KT

    def full_tile(j, carry):
        tile(j, False)
        return carry

    lax.fori_loop(0, n_full, full_tile, 0)
    tile(n_full, True)
    o_t = acc_ref[...] * (1.0 / l_ref[...])
    o_s = fold(jnp.concatenate([o_t[:, r * QB:(r + 1) * QB].T for r in range(R)], axis=0))

    kwin = jnp.concatenate([kw0[0, 0], kw1[0, 0], kw2[0, 0], kw3[0, 0], kw4[0, 0]], axis=0)
    vwin = jnp.concatenate([vw0[0, 0], vw1[0, 0], vw2[0, 0], vw3[0, 0], vw4[0, 0]], axis=0)
    NW = WINDOW + QB
    kpos = q0 - WINDOW + lax.broadcasted_iota(jnp.int32, (1, NW), 1)
    bias_w = bw_ref[...] + jnp.where(kpos >= 0, 0.0, NEG)
    s_w = _dot_nt(q4h, kwin) + rows4(bias_w)
    e_w = jnp.exp2(s_w - jnp.max(s_w, axis=-1, keepdims=True))
    r_w = 1.0 / jnp.sum(e_w, axis=-1, keepdims=True)
    o_w = fold(_dot(e_w.astype(BF16), vwin) * r_w)

    o_ref[0] = g0_ref[0] * o_c + g1_ref[0] * o_s + g2_ref[0] * o_w


def nsa_attention(qh, ql, kch, kcl, vc, ks, vs, kw, vw, gates):
    batch, seq, W = qh.shape
    G, R, d = NSA_GROUPS, NSA_REP, NSA_HEAD_DIM
    RW = R * d
    KL = KV_REP * d
    n_chunk = kch.shape[2]
    n_cmp = n_chunk - CMP_LEN // CMP_STRIDE + 1
    n_sel = seq // SEL_LEN
    n = np.arange(n_chunk)[:, None]
    j = np.arange(n_sel)[None, :]
    overlap = ((n * CMP_STRIDE <= j * SEL_LEN + SEL_LEN - 1) & (n * CMP_STRIDE + CMP_LEN - 1 >= j * SEL_LEN)
               & (n < n_cmp)).astype(np.float32)
    e0 = ((np.arange(SEL_TILE_BLOCKS * SEL_LEN)[:, None] // SEL_LEN) == np.arange(KL)[None, :]).astype(np.float32)
    dist = np.arange(Q_BLOCK)[:, None] + WINDOW - np.arange(WINDOW + Q_BLOCK)[None, :]
    bias_w = np.where((dist >= 0) & (dist < WINDOW), 0.0, NEG).astype(np.float32)
    qblk = pl.BlockSpec((1, Q_BLOCK, RW), lambda b, g, i: (b, i, g))
    cblk = pl.BlockSpec((1, 1, n_chunk, KL), lambda b, g, i: (b, g, 0, 0))
    sblk = pl.BlockSpec((1, 1, seq, KL), lambda b, g, i: (b, g, 0, 0))
    wblk = lambda o: pl.BlockSpec((1, 1, Q_BLOCK, KL), lambda b, g, i: (b, g, i + o, 0))
    gblk = lambda c: pl.BlockSpec((1, Q_BLOCK, RW), lambda b, g, i: (b, i, c * G + g))
    return pl.pallas_call(
        functools.partial(_nsa_attn_body, n_cmp=n_cmp),
        grid=(batch, G, seq // Q_BLOCK),
        in_specs=[qblk, qblk, cblk, cblk, cblk, sblk, pl.BlockSpec((1, 1, KL, seq), lambda b, g, i: (b, g, 0, 0))]
                 + [wblk(o) for o in range(5)] * 2
                 + [gblk(0), gblk(1), gblk(2), _full((n_chunk, n_sel)), _full((SEL_TILE_BLOCKS * SEL_LEN, KL)),
                    _full((Q_BLOCK, WINDOW + Q_BLOCK))],
        out_specs=qblk,
        out_shape=jax.ShapeDtypeStruct((batch, seq, W), F32),
        scratch_shapes=[pltpu.VMEM((1, R * Q_BLOCK), F32), pltpu.VMEM((1, R * Q_BLOCK), F32),
                        pltpu.VMEM((KL, R * Q_BLOCK), F32)],
        compiler_params=_cparams("parallel", "parallel", "arbitrary"),
        name="nsa_attention",
    )(qh, ql, kch, kcl, vc, ks, vs, *([kw] * 5), *([vw] * 5), gates, gates, gates,
      jnp.asarray(overlap, BF16), jnp.asarray(e0, BF16), jnp.asarray(bias_w))


def nsa_mix(z_rest3, p, cos, sin):
    batch, seq, _ = z_rest3.shape
    G, d = NSA_GROUPS, NSA_HEAD_DIM
    W = NSA_HEADS * d
    KW = G * d
    z2 = z_rest3.reshape(batch * seq, -1)
    qh, ql, ks, kw, gates = nsa_prep(z2, p, seq, cos, sin)
    base = 2 * SEG + W
    raw = lambda i: z_rest3[..., base + i * KW:base + (i + 1) * KW]
    kch, kcl, vc = nsa_compress(raw(0), raw(1), p, batch, seq)

    def rep(t, pad):
        t = t.reshape(batch, seq, G, d).transpose(0, 2, 1, 3).astype(BF16)
        t = jnp.tile(t, (1, 1, 1, KV_REP))
        return jnp.pad(t, ((0, 0), (0, 0), (pad, 0), (0, 0))) if pad else t

    return nsa_attention(qh.reshape(batch, seq, W), ql.reshape(batch, seq, W), kch, kcl, vc,
                         rep(ks.reshape(batch, seq, KW), 0), rep(raw(3), 0).transpose(0, 1, 3, 2),
                         rep(kw.reshape(batch, seq, KW), WINDOW), rep(raw(5), WINDOW),
                         gates.reshape(batch, seq, 3 * W))


def _merge_body(h_ref, zg_ref, ya_ref, yb_ref, yc_ref, pa_ref, pb_ref, pc_ref, wo_ref, o_ref):
    D = D_MODEL

    def branch(i, y_ref, p_ref):
        gate = jax.nn.sigmoid(zg_ref[:, i * D:(i + 1) * D].astype(F32))
        return gate * _dot(y_ref[...].astype(BF16), p_ref[...])

    merged = branch(0, ya_ref, pa_ref) + branch(1, yb_ref, pb_ref) + branch(2, yc_ref, pc_ref)
    o_ref[...] = h_ref[...] + _dot(merged.astype(BF16), wo_ref[...])


def merge_out(h, z_gate, y_rw, y_gla, y_nsa, p, tm=256):
    T, D = h.shape
    W = y_rw.shape[1]
    row = lambda w: pl.BlockSpec((tm, w), lambda i: (i, 0))
    return pl.pallas_call(
        _merge_body,
        grid=(T // tm,),
        in_specs=[row(D), row(3 * D), row(W), row(W), row(W),
                  _full((W, D)), _full((W, D)), _full((W, D)), _full((D, D))],
        out_specs=row(D),
        out_shape=jax.ShapeDtypeStruct((T, D), F32),
        compiler_params=_cparams("parallel"),
        name="merge_out",
    )(h, z_gate, y_rw, y_gla, y_nsa, p['proj_a'].astype(BF16), p['proj_b'].astype(BF16),
      p['proj_c'].astype(BF16), p['w_out'].astype(BF16))


def _ffn_body(h_ref, g_ref, wg_ref, wu_ref, wd_ref, o_ref, xn_ref, acc_ref):
    f = pl.program_id(1)

    @pl.when(f == 0)
    def _():
        x = h_ref[...]
        ms = jnp.mean(x * x, axis=-1, keepdims=True)
        xn_ref[...] = (x * lax.rsqrt(ms + NORM_EPS) * g_ref[...]).astype(BF16)
        acc_ref[...] = jnp.zeros_like(acc_ref)

    xn = xn_ref[...]
    a = jax.nn.silu(_dot(xn, wg_ref[...])) * _dot(xn, wu_ref[...])
    acc_ref[...] += _dot(a.astype(BF16), wd_ref[...])

    @pl.when(f == pl.num_programs(1) - 1)
    def _():
        o_ref[...] = h_ref[...] + acc_ref[...]


def ffn_dense(h, p, tm=512, tf=512):
    T, D = h.shape
    F = p['ffn_gate'].shape[1]
    return pl.pallas_call(
        _ffn_body,
        grid=(T // tm, F // tf),
        in_specs=[pl.BlockSpec((tm, D), lambda i, f: (i, 0)), _full((1, D)),
                  pl.BlockSpec((D, tf), lambda i, f: (0, f)), pl.BlockSpec((D, tf), lambda i, f: (0, f)),
                  pl.BlockSpec((tf, D), lambda i, f: (f, 0))],
        out_specs=pl.BlockSpec((tm, D), lambda i, f: (i, 0)),
        out_shape=jax.ShapeDtypeStruct((T, D), F32),
        scratch_shapes=[pltpu.VMEM((tm, D), BF16), pltpu.VMEM((tm, D), F32)],
        compiler_params=_cparams("parallel", "arbitrary"),
        name="ffn_dense",
    )(h, p['norm_ffn'].reshape(1, D), p['ffn_gate'].astype(BF16), p['ffn_up'].astype(BF16),
      p['ffn_down'].astype(BF16))


def _router_body(h_ref, g_ref, wr_ref, xn_ref, route_ref):
    x = h_ref[...]
    ms = jnp.mean(x * x, axis=-1, keepdims=True)
    xn = x * lax.rsqrt(ms + NORM_EPS) * g_ref[...]
    xn_ref[...] = xn.astype(BF16)
    logits = _dot(xn, wr_ref[...], HI)
    lane = lax.broadcasted_iota(jnp.int32, logits.shape, 1)
    valid = lane < N_EXPERTS
    probs = _masked_softmax(logits, valid)
    p1 = jnp.max(probs, axis=-1, keepdims=True)
    i1 = jnp.min(jnp.where(valid & (probs == p1), lane, LANES), axis=-1, keepdims=True)
    rest = jnp.where(valid & (lane != i1), probs, -1.0)
    p2 = jnp.max(rest, axis=-1, keepdims=True)
    i2 = jnp.min(jnp.where(rest == p2, lane, LANES), axis=-1, keepdims=True)
    tot = p1 + p2
    route_ref[...] = jnp.where(lane == 0, i1.astype(F32),
                               jnp.where(lane == 1, i2.astype(F32),
                                         jnp.where(lane == 2, p1 / tot, jnp.where(lane == 3, p2 / tot, 0.0))))


def moe_router(h, p, tm=512):
    T, D = h.shape
    wr = jnp.zeros((D, LANES), F32).at[:, :N_EXPERTS].set(p['router'])
    return pl.pallas_call(
        _router_body,
        grid=(T // tm,),
        in_specs=[pl.BlockSpec((tm, D), lambda i: (i, 0)), _full((1, D)), _full((D, LANES))],
        out_specs=[pl.BlockSpec((tm, D), lambda i: (i, 0)), pl.BlockSpec((tm, LANES), lambda i: (i, 0))],
        out_shape=[jax.ShapeDtypeStruct((T, D), BF16), jax.ShapeDtypeStruct((T, LANES), F32)],
        compiler_params=_cparams("parallel"),
        name="moe_router",
    )(h, p['norm_ffn'].reshape(1, D), wr)


def _moe_body(te_ref, tv_ref, x_ref, wg_ref, wu_ref, wd_ref, o_ref, acc_ref):
    t = pl.program_id(0)
    f = pl.program_id(1)

    @pl.when(f == 0)
    def _():
        acc_ref[...] = jnp.zeros_like(acc_ref)

    @pl.when(tv_ref[t] > 0)
    def _():
        x = x_ref[...]
        a = jax.nn.silu(_dot(x, wg_ref[0])) * _dot(x, wu_ref[0])
        acc_ref[...] += _dot(a.astype(BF16), wd_ref[0])

    @pl.when(f == pl.num_programs(1) - 1)
    def _():
        o_ref[...] = acc_ref[...]


def moe_grouped(x_sorted, tile_expert, tile_valid, p, tm, tf=1024):
    P, D = x_sorted.shape
    F = p['exp_gate'].shape[2]
    nf = F // tf

    def fidx(f, tv, t):
        return jnp.where(tv[t] > 0, f, nf - 1)

    grid_spec = pltpu.PrefetchScalarGridSpec(
        num_scalar_prefetch=2,
        grid=(P // tm, nf),
        in_specs=[pl.BlockSpec((tm, D), lambda t, f, te, tv: (t, 0)),
                  pl.BlockSpec((1, D, tf), lambda t, f, te, tv: (te[t], 0, fidx(f, tv, t))),
                  pl.BlockSpec((1, D, tf), lambda t, f, te, tv: (te[t], 0, fidx(f, tv, t))),
                  pl.BlockSpec((1, tf, D), lambda t, f, te, tv: (te[t], fidx(f, tv, t), 0))],
        out_specs=pl.BlockSpec((tm, D), lambda t, f, te, tv: (t, 0)),
        scratch_shapes=[pltpu.VMEM((tm, D), F32)],
    )
    return pl.pallas_call(
        _moe_body,
        grid_spec=grid_spec,
        out_shape=jax.ShapeDtypeStruct((P, D), F32),
        compiler_params=_cparams("arbitrary", "arbitrary"),
        name="moe_grouped",
    )(tile_expert, tile_valid, x_sorted, p['exp_gate'].astype(BF16), p['exp_up'].astype(BF16),
      p['exp_down'].astype(BF16))


def _combine_body(h_ref, ya_ref, yb_ref, route_ref, o_ref):
    w = route_ref[...]
    o_ref[...] = h_ref[...] + w[:, 2:3] * ya_ref[...] + w[:, 3:4] * yb_ref[...]


def moe_combine(h, ya, yb, route, tm=512):
    T, D = h.shape
    row = lambda w: pl.BlockSpec((tm, w), lambda i: (i, 0))
    return pl.pallas_call(
        _combine_body,
        grid=(T // tm,),
        in_specs=[row(D), row(D), row(D), row(LANES)],
        out_specs=row(D),
        out_shape=jax.ShapeDtypeStruct((T, D), F32),
        compiler_params=_cparams("parallel"),
        name="moe_combine",
    )(h, ya, yb, route)


def moe_ffn(h, p, tm=512):
    T, D = h.shape
    xn, route = moe_router(h, p)
    experts = route[:, 0:TOP_K].astype(jnp.int32).reshape(-1)
    n_pairs = T * TOP_K
    n_tiles = n_pairs // tm + N_EXPERTS
    counts = jnp.zeros((N_EXPERTS,), jnp.int32).at[experts].add(1)
    tiles_e = (counts + tm - 1) // tm
    tile_end = jnp.cumsum(tiles_e)
    row_start = (tile_end - tiles_e) * tm
    order = jnp.argsort(experts, stable=True)
    sorted_e = experts[order]
    rank = jnp.arange(n_pairs, dtype=jnp.int32) - (jnp.cumsum(counts) - counts)[sorted_e]
    dest_sorted = row_start[sorted_e] + rank
    dest = jnp.zeros((n_pairs,), jnp.int32).at[order].set(dest_sorted)
    row_token = jnp.zeros((n_tiles * tm,), jnp.int32).at[dest_sorted].set(order // TOP_K)
    tiles = jnp.arange(n_tiles, dtype=jnp.int32)
    tile_valid = (tiles < tile_end[-1]).astype(jnp.int32)
    tile_expert = jnp.minimum(jnp.searchsorted(tile_end, tiles, side='right'), N_EXPERTS - 1).astype(jnp.int32)
    last_e = tile_expert[jnp.maximum(tile_end[-1] - 1, 0)]
    tile_expert = jnp.where(tile_valid > 0, tile_expert, last_e)
    y_sorted = moe_grouped(xn[row_token], tile_expert, tile_valid, p, tm)
    dest = dest.reshape(T, TOP_K)
    return moe_combine(h, y_sorted[dest[:, 0]], y_sorted[dest[:, 1]], route)


def _reorder_w_in(w_in, has_vres):
    D = D_MODEL
    n_rw = 3 * RW_WIDTH + 160 + (32 if has_vres else 0)
    rw0 = 3 * D
    gla0 = rw0 + n_rw
    nsa0 = gla0 + 1552
    col = lambda a, b: w_in[:, a:b]
    zeros = lambda n: jnp.zeros((D, n), w_in.dtype)
    small = [col(rw0 + 1536, rw0 + n_rw)] + ([] if has_vres else [zeros(32)])
    small += [col(gla0 + 1024, gla0 + 1040), col(nsa0 + 1280, nsa0 + 1304), zeros(SMALL - S_NSA - 24)]
    rest = [col(rw0, rw0 + 1536), col(gla0, gla0 + 1024), col(gla0 + 1040, gla0 + 1552),
            col(nsa0, nsa0 + 1280)] + small
    return w_in[:, :rw0].astype(BF16), jnp.concatenate(rest, axis=1).astype(BF16)


def _layer(h, p, v_first, batch, seq, cos, sin, is_moe):
    T = batch * seq
    w_gate, w_rest = _reorder_w_in(p['w_in'], v_first is not None)
    z_gate = norm_matmul(h, p['norm_mix'], w_gate, BF16, 512, 768)
    z_rest = norm_matmul(h, p['norm_mix'], w_rest, F32, 512, 768)
    z_rest3 = z_rest.reshape(batch, seq, -1)
    prep = rwkv_prep(z_rest, v_first, p, seq)
    if v_first is None:
        v_first = prep[3]
    y_rw = rwkv_chunk(prep, p, batch, seq).reshape(T, -1)
    y_gla = gla_mix(z_rest3, p).reshape(T, -1)
    y_nsa = nsa_mix(z_rest3, p, cos, sin).reshape(T, -1)
    h = merge_out(h, z_gate, y_rw, y_gla, y_nsa, p)
    h = moe_ffn(h, p) if is_moe else ffn_dense(h, p)
    return h, v_first


_L0 = ('norm_mix', 'w_in', 'rw_mu', 'rw_w0', 'rw_w2', 'rw_a0', 'rw_a2', 'rw_g2', 'rw_kk', 'rw_ka', 'rw_rk',
       'rw_ln_w', 'rw_ln_b', 'gla_wg', 'gla_bg', 'gla_norm', 'nsa_qnorm', 'nsa_kcnorm', 'nsa_ksnorm',
       'nsa_kwnorm', 'nsa_pe_k', 'nsa_c1_k', 'nsa_c2_k', 'nsa_pe_v', 'nsa_c1_v', 'nsa_c2_v', 'proj_a', 'proj_b',
       'proj_c', 'w_out', 'norm_ffn', 'ffn_gate', 'ffn_up', 'ffn_down')
_L1 = ('norm_mix', 'w_in', 'rw_mu', 'rw_w0', 'rw_w2', 'rw_a0', 'rw_a2', 'rw_g2', 'rw_v0', 'rw_v2', 'rw_kk',
       'rw_ka', 'rw_rk', 'rw_ln_w', 'rw_ln_b', 'gla_wg', 'gla_bg', 'gla_norm', 'nsa_qnorm', 'nsa_kcnorm',
       'nsa_ksnorm', 'nsa_kwnorm', 'nsa_pe_k', 'nsa_c1_k', 'nsa_c2_k', 'nsa_pe_v', 'nsa_c1_v', 'nsa_c2_v',
       'proj_a', 'proj_b', 'proj_c', 'w_out', 'norm_ffn', 'router', 'exp_gate', 'exp_up', 'exp_down')


def kernel(x, *weights):
    p0 = dict(zip(_L0, weights[:len(_L0)]))
    p1 = dict(zip(_L1, weights[len(_L0):]))
    batch, seq, D = x.shape
    cos, sin = _rope_tables(jnp.arange(seq), NSA_HEADS * NSA_HEAD_DIM)
    h = x.reshape(batch * seq, D)
    h, v_first = _layer(h, p0, None, batch, seq, cos, sin, False)
    h, _ = _layer(h, p1, v_first, batch, seq, cos, sin, True)
    return h.reshape(batch, seq, D)
```

```python
import functools

import numpy as np
import jax
import jax.numpy as jnp
from jax import lax
from jax.experimental import pallas as pl
from jax.experimental.pallas import tpu as pltpu

F32 = jnp.float32
BF16 = jnp.bfloat16
HI = lax.Precision.HIGHEST

D_MODEL = 2048
RW_HEADS = 8
RW_HEAD_DIM = 64
RW_WIDTH = 512
RW_GN_EPS = 64e-5
GLA_HEADS = 4
GLA_DK = 64
GLA_DV = 128
GLA_TAU = 16.0
CHUNK = 64
NSA_HEADS = 8
NSA_GROUPS = 2
NSA_REP = NSA_HEADS // NSA_GROUPS
NSA_HEAD_DIM = 64
KV_REP = 2
SEL_TILE_BLOCKS = 16
NSA_VROWS = NSA_HEAD_DIM + 16
CMP_LEN = 32
CMP_STRIDE = 16
SEL_LEN = 64
SEL_TOPK = 16
WINDOW = 512
Q_BLOCK = 128
ROPE_THETA = 10000.0
N_EXPERTS = 8
TOP_K = 2
NORM_EPS = 1e-6

LANES = 128
VMEM_LIMIT = 56 * 1024 * 1024

SEG = 1536
SMALL = 256
SMALL_OFF = 3 * SEG - SMALL
S_ZW, S_ZA, S_ZG, S_ZV, S_GLA, S_NSA = 0, 32, 64, 160, 192, 208
NEG = -1e30
LOG2E = 1.4426950408889634


def _dot(a, b, precision=None):
    return jnp.dot(a, b, preferred_element_type=F32, precision=precision)


def _dot_nt(a, b, precision=None):
    return lax.dot_general(a, b, (((1,), (1,)), ((), ())), preferred_element_type=F32, precision=precision)


def _dot_tn(a, b, precision=None):
    return lax.dot_general(a, b, (((0,), (0,)), ((), ())), preferred_element_type=F32, precision=precision)


def _split_bf16(x):
    hi = x.astype(BF16)
    return hi, (x - hi.astype(F32)).astype(BF16)


def _dot_rhs_exact(a, b):
    hi, lo = _split_bf16(a)
    return _dot(hi, b) + _dot(lo, b)


def _dot_lhs_exact(a, b):
    hi, lo = _split_bf16(b)
    return _dot(a, hi) + _dot(a, lo)


def _dot_split(a, b_hi, b_lo):
    hi, lo = _split_bf16(a)
    return _dot(hi, b_hi) + _dot(hi, b_lo) + _dot(lo, b_hi)


def _cparams(*sem, flags=None):
    return pltpu.CompilerParams(dimension_semantics=sem, vmem_limit_bytes=VMEM_LIMIT, flags=flags)


def _full(shape):
    n = len(shape)
    return pl.BlockSpec(shape, lambda *_: (0,) * n)


def _norm_matmul_body(x_ref, g_ref, w_ref, o_ref, xn_ref):
    @pl.when(pl.program_id(1) == 0)
    def _():
        x = x_ref[...]
        ms = jnp.mean(x * x, axis=-1, keepdims=True)
        xn_ref[...] = (x * lax.rsqrt(ms + NORM_EPS) * g_ref[...]).astype(BF16)

    o_ref[...] = _dot(xn_ref[...], w_ref[...]).astype(o_ref.dtype)


def norm_matmul(x, g, w, out_dtype, tm, tn):
    M, K = x.shape
    N = w.shape[1]
    return pl.pallas_call(
        _norm_matmul_body,
        grid=(M // tm, N // tn),
        in_specs=[pl.BlockSpec((tm, K), lambda i, j: (i, 0)),
                  pl.BlockSpec((1, K), lambda i, j: (0, 0)),
                  pl.BlockSpec((K, tn), lambda i, j: (0, j))],
        out_specs=pl.BlockSpec((tm, tn), lambda i, j: (i, j)),
        out_shape=jax.ShapeDtypeStruct((M, N), out_dtype),
        scratch_shapes=[pltpu.VMEM((tm, K), BF16)],
        compiler_params=_cparams("parallel", "arbitrary"),
        name="norm_matmul",
    )(x, g.reshape(1, K), w)


def _shift_rows(x, prev_row):
    rolled = pltpu.roll(x, 1, axis=0)
    first = lax.broadcasted_iota(jnp.int32, x.shape, 0) == 0
    return jnp.where(first, prev_row, rolled)


def _rwkv_prep_body(*refs, tiles_per_seq, has_vres):
    if has_vres:
        (zm_ref, zs_ref, vf_ref, mum_ref, mus_ref, vec_ref, w2_ref, a2_ref, g2_ref, v2_ref, bd_ref,
         r_ref, lw_ref, k_ref, v_ref, kk_ref, b_ref, g_ref, bonus_ref, pm_ref, ps_ref) = refs
    else:
        (zm_ref, zs_ref, mum_ref, mus_ref, vec_ref, w2_ref, a2_ref, g2_ref, bd_ref,
         r_ref, lw_ref, k_ref, v_ref, kk_ref, b_ref, g_ref, bonus_ref, pm_ref, ps_ref) = refs
    i = pl.program_id(0)

    @pl.when(i % tiles_per_seq == 0)
    def _():
        pm_ref[...] = jnp.zeros_like(pm_ref)
        ps_ref[...] = jnp.zeros_like(ps_ref)

    zm = zm_ref[...]
    zs = zs_ref[...]
    tm = zm.shape[0]
    zm_prev = _shift_rows(zm, pm_ref[0:1, :])
    zs_prev = _shift_rows(zs, ps_ref[0:1, :])
    pm_ref[0:1, :] = zm[tm - 1:tm, :]
    ps_ref[0:1, :] = zs[tm - 1:tm, :]
    zm = zm + (zm_prev - zm) * mum_ref[...]
    zs = zs + (zs_prev - zs) * mus_ref[...]

    r = zm[:, 0:RW_WIDTH]
    k = zm[:, RW_WIDTH:2 * RW_WIDTH]
    v = zm[:, 2 * RW_WIDTH:3 * RW_WIDTH]
    w0 = vec_ref[0:1, :]
    a0 = vec_ref[1:2, :]
    v0 = vec_ref[2:3, :]
    kkw = vec_ref[3:4, :]
    ka = vec_ref[4:5, :]
    rk = vec_ref[5:6, :]

    w_log = -jax.nn.softplus(-(w0 + _dot_split(jnp.tanh(zs), w2_ref[0], w2_ref[1]))) - 0.5
    lw = -jnp.exp(w_log)
    a = jax.nn.sigmoid(a0 + _dot_split(zs, a2_ref[0], a2_ref[1]))
    g = _dot_split(jax.nn.sigmoid(zs), g2_ref[0], g2_ref[1])
    if has_vres:
        v = v + (vf_ref[...] - v) * jax.nn.sigmoid(v0 + _dot_split(zs, v2_ref[0], v2_ref[1]))

    bd = bd_ref[...]
    kk = k * kkw
    nrm = jnp.sqrt(_dot_rhs_exact(kk * kk, bd))
    kk = kk / jnp.maximum(nrm, 1e-12)
    k2 = k * (1.0 + (a - 1.0) * ka)
    bonus = _dot_rhs_exact(r * k2 * rk, bd) * v

    r_ref[...] = r
    lw_ref[...] = lw
    k_ref[...] = k2
    v_ref[...] = v
    kk_ref[...] = kk
    b_ref[...] = kk * a
    g_ref[...] = g
    bonus_ref[...] = bonus


def _pad_rows(w, off, total=SMALL):
    hi, lo = _split_bf16(jnp.zeros((total, w.shape[1]), F32).at[off:off + w.shape[0]].set(w))
    return jnp.stack([hi, lo])


def rwkv_prep(z_rest, v_first, p, seq, tm=256):
    T = z_rest.shape[0]
    has_vres = v_first is not None
    mu = p['rw_mu']
    n_small = mu.shape[0] - 3 * RW_WIDTH
    mum = mu[:3 * RW_WIDTH].reshape(1, -1)
    mus = jnp.zeros((SMALL,), F32).at[:n_small].set(mu[3 * RW_WIDTH:]).reshape(1, -1)
    zero = jnp.zeros((RW_WIDTH,), F32)
    vec = jnp.stack([p['rw_w0'], p['rw_a0'], p['rw_v0'] if has_vres else zero, p['rw_kk'], p['rw_ka'],
                     p['rw_rk'].reshape(-1), zero, zero])
    w2 = _pad_rows(p['rw_w2'], S_ZW)
    a2 = _pad_rows(p['rw_a2'], S_ZA)
    g2 = _pad_rows(p['rw_g2'], S_ZG)
    hid = np.arange(RW_WIDTH) // RW_HEAD_DIM
    bd = jnp.asarray((hid[:, None] == hid[None, :]).astype(np.float32), BF16)
    row = lambda w: pl.BlockSpec((tm, w), lambda i: (i, 0))
    lora = _full((2, SMALL, RW_WIDTH))
    in_specs = [row(SEG), pl.BlockSpec((tm, SMALL), lambda i: (i, SMALL_OFF // SMALL))]
    args = [z_rest, z_rest]
    if has_vres:
        in_specs.append(row(RW_WIDTH))
        args.append(v_first)
    in_specs += [_full((1, SEG)), _full((1, SMALL)), _full((8, RW_WIDTH)), lora, lora, lora]
    args += [mum, mus, vec, w2, a2, g2]
    if has_vres:
        in_specs.append(lora)
        args.append(_pad_rows(p['rw_v2'], S_ZV))
    in_specs.append(_full((RW_WIDTH, RW_WIDTH)))
    args.append(bd)
    out = jax.ShapeDtypeStruct((T, RW_WIDTH), F32)
    return pl.pallas_call(
        functools.partial(_rwkv_prep_body, tiles_per_seq=seq // tm, has_vres=has_vres),
        grid=(T // tm,),
        in_specs=in_specs,
        out_specs=[row(RW_WIDTH)] * 8,
        out_shape=[out] * 8,
        scratch_shapes=[pltpu.VMEM((8, SEG), F32), pltpu.VMEM((8, SMALL), F32)],
        compiler_params=_cparams("arbitrary"),
        name="rwkv_prep",
    )(*args)


def _rwkv_chunk_body(r_ref, lw_ref, k_ref, v_ref, kk_ref, b_ref, g_ref, bonus_ref, lnw_ref, lnb_ref,
                     o_ref, state_ref, *, n_chunks):
    C = CHUNK
    N = RW_HEAD_DIM

    @pl.when(pl.program_id(1) == 0)
    def _():
        state_ref[...] = jnp.zeros_like(state_ref)

    ti = lax.broadcasted_iota(jnp.int32, (C, C), 0)
    si = lax.broadcasted_iota(jnp.int32, (C, C), 1)
    incl = si <= ti
    strict = si < ti
    tril1 = incl.astype(BF16)
    eye = (si == ti).astype(F32)
    masks = []
    m = 1
    while m < C:
        masks.append(((ti // (2 * m)) == (si // (2 * m))) & ((ti // m) % 2 == 1) & ((si // m) % 2 == 0))
        m *= 2
    lnw = lnw_ref[...]
    lnb = lnb_ref[...]

    NCH = range(n_chunks)
    H = range(RW_HEADS)
    U = [(c, h) for c in NCH for h in H]
    sl = [slice(h * N, (h + 1) * N) for h in H]
    ew = []
    for c in NCH:
        rows = pl.ds(c * C, C)
        lw = lw_ref[0, rows, :]
        k = k_ref[0, rows, :]
        b = b_ref[0, rows, :]
        cum = _dot_lhs_exact(tril1, lw)
        cum_last = cum[C - 1:C, :]
        e_neg = jnp.exp(-cum)
        e_rel = jnp.exp(cum_last - cum)
        ew.append(dict(
            rows=rows,
            alpha=(kk_ref[0, rows, :] * jnp.exp(cum - lw)).astype(BF16),
            beta=(b * e_neg).astype(BF16), kappa=(k * e_neg).astype(BF16),
            rho=(r_ref[0, rows, :] * jnp.exp(cum)).astype(BF16),
            beta2=(b * e_rel).astype(BF16), kappa2=(k * e_rel).astype(BF16),
            pc=jnp.exp(cum_last), v=v_ref[0, rows, :].astype(BF16)))
    A = [ew[c]['alpha'][:, sl[h]] for c, h in U]
    Rh = [ew[c]['rho'][:, sl[h]] for c, h in U]
    V = [ew[c]['v'][:, sl[h]] for c, h in U]
    p4 = [_dot_nt(jnp.concatenate([A[u], Rh[u]], axis=0),
                  jnp.concatenate([ew[c]['beta'][:, sl[h]], ew[c]['kappa'][:, sl[h]]], axis=0))
          for u, (c, h) in enumerate(U)]
    L = [jnp.where(strict, p[:C, :C], 0.0) for p in p4]
    T = [eye for _ in U]
    for mk in masks:
        n = [_dot(T[u].astype(BF16), jnp.where(mk, L[u], 0.0).astype(BF16)) for u in range(len(U))]
        T = [T[u] - _dot(n[u].astype(BF16), T[u].astype(BF16)) for u in range(len(U))]
    T = [t.astype(BF16) for t in T]
    kv = [_dot(jnp.concatenate([jnp.where(strict, p[:C, C:], 0.0), jnp.where(incl, p[C:, C:], 0.0)],
                               axis=0).astype(BF16), V[u]) for u, p in enumerate(p4)]
    lhs = [jnp.concatenate([_dot(T[u], A[u]).astype(BF16), Rh[u]], axis=0) for u in range(len(U))]
    W = [_dot(T[u], kv[u][:C].astype(BF16)) for u in range(len(U))]
    D = [_dot_tn(V[u], ew[c]['kappa2'][:, sl[h]]) for u, (c, h) in enumerate(U)]
    G = [jnp.where(incl, p[C:, :C], 0.0).astype(BF16) for p in p4]

    S = [state_ref[h] for h in H]
    for c in NCH:
        u0 = c * RW_HEADS
        rs = [_dot_nt(lhs[u0 + h], S[h].astype(BF16)) for h in H]
        Ub = [(-(rs[h][:C] + W[u0 + h])).astype(BF16) for h in H]
        Y = [rs[h][C:] + _dot(G[u0 + h], Ub[h]) + kv[u0 + h][C:] for h in H]
        S = [S[h] * ew[c]['pc'][:, sl[h]] + _dot_tn(Ub[h], ew[c]['beta2'][:, sl[h]]) + D[u0 + h] for h in H]
        yn = []
        for h in H:
            mu = jnp.mean(Y[h], axis=-1, keepdims=True)
            yc = Y[h] - mu
            var = jnp.mean(yc * yc, axis=-1, keepdims=True)
            yn.append(yc * lax.rsqrt(var + RW_GN_EPS))
        rows = ew[c]['rows']
        o_ref[0, rows, :] = ((jnp.concatenate(yn, axis=1) * lnw + lnb + bonus_ref[0, rows, :])
                             * g_ref[0, rows, :])
    state_ref[...] = jnp.stack(S)


def rwkv_chunk(prep, p, batch, seq, rows=256):
    r, lw, k2, v, kk, b, g, bonus = [t.reshape(batch, seq, RW_WIDTH) for t in prep]
    blk = pl.BlockSpec((1, rows, RW_WIDTH), lambda bi, i: (bi, i, 0))
    return pl.pallas_call(
        functools.partial(_rwkv_chunk_body, n_chunks=rows // CHUNK),
        grid=(batch, seq // rows),
        in_specs=[blk] * 8 + [_full((1, RW_WIDTH))] * 2,
        out_specs=blk,
        out_shape=jax.ShapeDtypeStruct((batch, seq, RW_WIDTH), F32),
        scratch_shapes=[pltpu.VMEM((RW_HEADS, RW_HEAD_DIM, RW_HEAD_DIM), F32)],
        compiler_params=_cparams("parallel", "arbitrary"),
        name="rwkv_chunk",
    )(r, lw, k2, v, kk, b, g, bonus, p['rw_ln_w'].reshape(1, -1), p['rw_ln_b'].reshape(1, -1))


def _gla_body(z_ref, zs_ref, wg_ref, bg_ref, gn_ref, o_ref, state_ref, *, n_chunks):
    C = CHUNK
    HK = GLA_HEADS * GLA_DK
    HV = GLA_HEADS * GLA_DV

    @pl.when(pl.program_id(1) == 0)
    def _():
        state_ref[...] = jnp.zeros_like(state_ref)

    RB = n_chunks * C
    ti = lax.broadcasted_iota(jnp.int32, (RB, RB), 0)
    si = lax.broadcasted_iota(jnp.int32, (RB, RB), 1)
    tril_blk = ((si <= ti) & (si // C == ti // C)).astype(BF16)
    causal = (lax.broadcasted_iota(jnp.int32, (C, HK), 1) % GLA_DK) <= lax.broadcasted_iota(jnp.int32, (C, HK), 0)
    blk_k = (lax.broadcasted_iota(jnp.int32, (HK, HK), 0) // GLA_DK) == (lax.broadcasted_iota(jnp.int32, (HK, HK), 1) // GLA_DK)
    blk_v = (lax.broadcasted_iota(jnp.int32, (HK, HV), 0) // GLA_DK) == (lax.broadcasted_iota(jnp.int32, (HK, HV), 1) // GLA_DV)
    blk_vt = (lax.broadcasted_iota(jnp.int32, (HV, HK), 0) // GLA_DV) == (lax.broadcasted_iota(jnp.int32, (HV, HK), 1) // GLA_DK)

    z = z_ref[0]
    log_a = jax.nn.log_sigmoid(_dot_split(zs_ref[0], wg_ref[0], wg_ref[1]) + bg_ref[...]) / GLA_TAU
    bcum_all = _dot_lhs_exact(tril_blk, log_a)
    pre = []
    for c in range(n_chunks):
        rows = slice(c * C, (c + 1) * C)
        q = z[rows, 0:HK] * (GLA_DK ** -0.5)
        k = z[rows, HK:2 * HK]
        vb = z[rows, 2 * HK:2 * HK + HV].astype(BF16)
        bcum = bcum_all[rows]
        b_last = bcum[C - 1:C, :]
        q_dec = (q * jnp.exp(bcum)).astype(BF16)
        k_inv = (k * jnp.exp(-bcum)).astype(BF16)
        k_dec = (k * jnp.exp(b_last - bcum)).astype(BF16)
        kbd = jnp.where(blk_k, jnp.concatenate([k_inv] * GLA_HEADS, axis=0), jnp.zeros((), BF16))
        att = jnp.where(causal, _dot_nt(q_dec, kbd), 0.0)
        vbd = jnp.where(blk_v, jnp.concatenate([vb] * GLA_HEADS, axis=0), jnp.zeros((), BF16))
        pre.append((q_dec, _dot(att.astype(BF16), vbd), jnp.exp(b_last),
                    jnp.where(blk_vt, _dot_tn(vb, k_dec), 0.0)))
    st = state_ref[...]
    outs = []
    for q_dec, o_intra, decay, inc in pre:
        outs.append(o_intra + _dot_nt(q_dec, st.astype(BF16)))
        st = st * decay + inc
    state_ref[...] = st
    o = jnp.concatenate(outs, axis=0)
    parts = []
    for h in range(GLA_HEADS):
        oh = o[:, h * GLA_DV:(h + 1) * GLA_DV]
        ms = jnp.mean(oh * oh, axis=-1, keepdims=True)
        parts.append(oh * lax.rsqrt(ms + NORM_EPS))
    on = jnp.concatenate(parts, axis=-1) * gn_ref[...]
    o_ref[0] = on * jax.nn.silu(z[:, 2 * HK + HV:2 * HK + 2 * HV])


def gla_mix(z_rest3, p, rows=256):
    batch, seq, _ = z_rest3.shape
    HV = GLA_HEADS * GLA_DV
    wg = _pad_rows(p['gla_wg'], S_GLA)
    return pl.pallas_call(
        functools.partial(_gla_body, n_chunks=rows // CHUNK),
        grid=(batch, seq // rows),
        in_specs=[pl.BlockSpec((1, rows, SEG), lambda b, i: (b, i, 1)),
                  pl.BlockSpec((1, rows, SMALL), lambda b, i: (b, i, SMALL_OFF // SMALL)),
                  _full((2, SMALL, GLA_HEADS * GLA_DK)), _full((1, GLA_HEADS * GLA_DK)), _full((1, HV))],
        out_specs=pl.BlockSpec((1, rows, HV), lambda b, i: (b, i, 0)),
        out_shape=jax.ShapeDtypeStruct((batch, seq, HV), F32),
        scratch_shapes=[pltpu.VMEM((HV, GLA_HEADS * GLA_DK), F32)],
        compiler_params=_cparams("parallel", "arbitrary"),
        name="gla_mix",
    )(z_rest3, z_rest3, wg, p['gla_bg'].reshape(1, -1), jnp.tile(p['gla_norm'], GLA_HEADS).reshape(1, -1))


def _rope(x, cos, sin_signed):
    w = x.shape[-1]
    lane = lax.broadcasted_iota(jnp.int32, x.shape, 1) % NSA_HEAD_DIM
    partner = jnp.where(lane < NSA_HEAD_DIM // 2, pltpu.roll(x, w - NSA_HEAD_DIM // 2, axis=1),
                        pltpu.roll(x, NSA_HEAD_DIM // 2, axis=1))
    return x * cos + partner * sin_signed


def _nsa_prep_body(z_ref, cos_ref, sin_ref, qn_ref, kn_ref, bd_ref,
                   qh_ref, ql_ref, ks_ref, kw_ref, vs_ref, vw_ref, g_ref, *, tiles_per_seq):
    z = z_ref[...]
    cos = cos_ref[...]
    sin = sin_ref[...]
    bd = bd_ref[...]
    d = NSA_HEAD_DIM
    W = NSA_HEADS * d
    KW = NSA_GROUPS * d
    tm = z.shape[0]

    def head_norm(x, g, n):
        ms = _dot_rhs_exact(x * x, bd[:n, :n]) * (1.0 / d)
        return x * lax.rsqrt(ms + NORM_EPS) * g

    q = head_norm(z[:, 0:W], qn_ref[...], W)
    qh_ref[...], ql_ref[...] = _split_bf16((_rope(q, cos, sin) * (d ** -0.5 * LOG2E)).T)

    lane = lax.broadcasted_iota(jnp.int32, (tm, LANES), 1)
    pos = (pl.program_id(0) % tiles_per_seq) * tm + lax.broadcasted_iota(jnp.int32, (tm, LANES), 0)
    onehot = jnp.where(lane - d == (pos // SEL_LEN) % SEL_TILE_BLOCKS, 1.0, 0.0)

    def keys(k, extra, ref):
        for g in range(NSA_GROUPS):
            kg = k if g == 0 else pltpu.roll(k, LANES - g * d, axis=1)
            ref[g] = jnp.where(lane < d, kg, extra).astype(BF16)

    def values_t(v, ref):
        vt = v.T
        tail = jnp.where(lax.broadcasted_iota(jnp.int32, (NSA_VROWS - d, tm), 0) == 0, 1.0, 0.0)
        for g in range(NSA_GROUPS):
            ref[g] = jnp.concatenate([vt[g * d:(g + 1) * d], tail], axis=0).astype(BF16)

    ks = head_norm(z[:, W + 2 * KW:W + 3 * KW], kn_ref[0:1, :], KW)
    keys(_rope(ks, cos[:, :KW], sin[:, :KW]), onehot, ks_ref)
    kw = head_norm(z[:, W + 4 * KW:W + 5 * KW], kn_ref[1:2, :], KW)
    keys(_rope(kw, cos[:, :KW], sin[:, :KW]), jnp.zeros((tm, LANES), F32), kw_ref)
    values_t(z[:, W + 3 * KW:W + 4 * KW], vs_ref)
    values_t(z[:, W + 5 * KW:W + 6 * KW], vw_ref)
    g_ref[...] = jax.nn.sigmoid(z[:, SEG - SMALL:SEG])


def _rope_tables(pos, width):
    inv = 1.0 / (ROPE_THETA ** (jnp.arange(0, NSA_HEAD_DIM, 2, dtype=F32) / NSA_HEAD_DIM))
    ang = pos.astype(F32)[:, None] * inv[None, :]
    c, s = jnp.cos(ang), jnp.sin(ang)
    reps = width // NSA_HEAD_DIM
    return jnp.tile(jnp.concatenate([c, c], -1), (1, reps)), jnp.tile(jnp.concatenate([-s, s], -1), (1, reps))


def nsa_prep(z_rest, p, seq, cos, sin, tm=256):
    T = z_rest.shape[0]
    W = NSA_HEADS * NSA_HEAD_DIM
    KW = NSA_GROUPS * NSA_HEAD_DIM
    hid = np.arange(W) // NSA_HEAD_DIM
    bd = jnp.asarray((hid[:, None] == hid[None, :]).astype(np.float32), BF16)
    qn = jnp.tile(p['nsa_qnorm'], NSA_HEADS).reshape(1, W)
    kn = jnp.stack([jnp.tile(p['nsa_ksnorm'], NSA_GROUPS), jnp.tile(p['nsa_kwnorm'], NSA_GROUPS)])
    tps = seq // tm
    tab = pl.BlockSpec((tm, W), lambda i: (i % tps, 0))
    G = NSA_GROUPS
    q_spec = pl.BlockSpec((W, tm), lambda i: (0, i))
    k_spec = pl.BlockSpec((G, tm, LANES), lambda i: (0, i, 0))
    v_spec = pl.BlockSpec((G, NSA_VROWS, tm), lambda i: (0, 0, i))
    q_shape = jax.ShapeDtypeStruct((W, T), BF16)
    k_shape = jax.ShapeDtypeStruct((G, T, LANES), BF16)
    v_shape = jax.ShapeDtypeStruct((G, NSA_VROWS, T), BF16)
    return pl.pallas_call(
        functools.partial(_nsa_prep_body, tiles_per_seq=tps),
        grid=(T // tm,),
        in_specs=[pl.BlockSpec((tm, SEG), lambda i: (i, 2)), tab, tab,
                  _full((1, W)), _full((2, KW)), _full((W, W))],
        out_specs=[q_spec, q_spec, k_spec, k_spec, v_spec, v_spec, pl.BlockSpec((tm, SMALL), lambda i: (i, 0))],
        out_shape=[q_shape, q_shape, k_shape, k_shape, v_shape, v_shape, jax.ShapeDtypeStruct((T, SMALL), F32)],
        compiler_params=_cparams("parallel"),
        name="nsa_prep",
    )(z_rest, cos, sin, qn, kn, bd)


def _nsa_compress_body(xk_ref, xv_ref, pek_ref, pev_ref, w1k_ref, w1v_ref, w2k_ref, w2v_ref, kn_ref,
                       cos_ref, sin_ref, kch_ref, kcl_ref, vc_ref):
    def compress(x, pe_ref, w1_ref, w2_ref):
        a = _dot((x + pe_ref[0:1, :]).astype(BF16), w1_ref[0])
        b = _dot((x + pe_ref[1:2, :]).astype(BF16), w1_ref[1])
        n = a.shape[0]
        h = jax.nn.gelu(a + pltpu.roll(b, n - 1, axis=0))
        return _dot(h.astype(BF16), w2_ref[...])

    kc = compress(xk_ref[0, 0], pek_ref, w1k_ref, w2k_ref)
    ms = jnp.mean(kc * kc, axis=-1, keepdims=True)
    kc = kc * lax.rsqrt(ms + NORM_EPS) * kn_ref[...]
    kc = kc * cos_ref[...] + pltpu.roll(kc, NSA_HEAD_DIM // 2, axis=1) * sin_ref[...]
    kch_ref[0, 0], kcl_ref[0, 0] = _split_bf16(kc)
    vc_ref[0, 0] = compress(xv_ref[0, 0], pev_ref, w1v_ref, w2v_ref).astype(BF16)


def nsa_compress(kc_raw, vc_raw, p, batch, seq):
    G, d = NSA_GROUPS, NSA_HEAD_DIM
    n_chunk = seq // CMP_STRIDE
    RW = KV_REP * d

    def chunks(t):
        t = t.reshape(batch, n_chunk, CMP_STRIDE, G, d).transpose(0, 3, 1, 2, 4)
        return t.reshape(batch, G, n_chunk, CMP_STRIDE * d)

    def w1(t):
        return t.reshape(2, CMP_STRIDE * d, d).astype(BF16)

    def pe(t):
        return t.reshape(2, CMP_STRIDE * d)

    c_end = jnp.arange(n_chunk) * CMP_STRIDE + (CMP_LEN - 1)
    cos_c, sin_c = _rope_tables(c_end, RW)
    blk = pl.BlockSpec((1, 1, n_chunk, CMP_STRIDE * d), lambda b, g: (b, g, 0, 0))
    oblk = pl.BlockSpec((1, 1, n_chunk, RW), lambda b, g: (b, g, 0, 0))
    return pl.pallas_call(
        _nsa_compress_body,
        grid=(batch, G),
        in_specs=[blk, blk, _full((2, CMP_STRIDE * d)), _full((2, CMP_STRIDE * d)),
                  _full((2, CMP_STRIDE * d, d)), _full((2, CMP_STRIDE * d, d)),
                  _full((d, RW)), _full((d, RW)), _full((1, RW)), _full((n_chunk, RW)), _full((n_chunk, RW))],
        out_specs=[oblk, oblk, oblk],
        out_shape=[jax.ShapeDtypeStruct((batch, G, n_chunk, RW), BF16)] * 3,
        compiler_params=_cparams("parallel", "parallel"),
        name="nsa_compress",
    )(chunks(kc_raw), chunks(vc_raw), pe(p['nsa_pe_k']), pe(p['nsa_pe_v']), w1(p['nsa_c1_k']), w1(p['nsa_c1_v']),
      jnp.tile(p['nsa_c2_k'], (1, KV_REP)).astype(BF16), jnp.tile(p['nsa_c2_v'], (1, KV_REP)).astype(BF16),
      jnp.tile(p['nsa_kcnorm'], KV_REP).reshape(1, RW), cos_c, sin_c)


def _masked_softmax(s, valid):
    m = jnp.max(jnp.where(valid, s, NEG), axis=-1, keepdims=True)
    e = jnp.where(valid, jnp.exp(s - m), 0.0)
    return e / jnp.maximum(jnp.sum(e, axis=-1, keepdims=True), jnp.finfo(F32).tiny)


def _nsa_attn_t_body(qh_ref, ql_ref, kch_ref, kcl_ref, lc_ref, ks_ref, vst_ref,
                     kw0, kw1, kw2, kw3, kw4, vw0, vw1, vw2, vw3, vw4,
                     g_ref, bw_ref, o_ref, mb_ref, sa_ref, sb_ref, *, n_cmp):
    QB = Q_BLOCK
    R = NSA_REP
    d = NSA_HEAD_DIM
    TB = SEL_TILE_BLOCKS
    KT = TB * SEL_LEN
    VROWS = NSA_VROWS
    q0 = pl.program_id(2) * QB
    tiny = jnp.finfo(F32).tiny

    def lanes4(x):
        return jnp.concatenate([x] * R, axis=1)

    def col_reduce(x, op):
        return op(x, axis=0, keepdims=True)

    def rhs(q_t, extra=None):
        parts = [jnp.concatenate([q_t[r * d:(r + 1) * d, :] for r in range(R)], axis=1)]
        if extra is not None:
            parts.append(extra)
        n = sum(x.shape[0] for x in parts)
        return jnp.concatenate(parts + [jnp.zeros((LANES - n, R * QB), BF16)], axis=0)

    qh_t = qh_ref[...]
    rhs_h = rhs(qh_t)
    rhs_l = rhs(ql_ref[...])

    NC = kch_ref.shape[2]
    NB = lc_ref.shape[2] - VROWS

    def compressed(nc):
        kch = kch_ref[0, 0, 0:nc, :]
        n_c = lax.broadcasted_iota(jnp.int32, (nc, QB), 0)
        t_c = q0 + lax.broadcasted_iota(jnp.int32, (nc, QB), 1)
        bias_c = jnp.where(n_c * CMP_STRIDE + (CMP_LEN - 1) <= t_c, jnp.where(n_c < n_cmp, 0.0, NEG), NEG)
        s_c = _dot(kch, rhs_h) + _dot(kcl_ref[0, 0, 0:nc, :], rhs_h) + _dot(kch, rhs_l) + lanes4(bias_c)
        m_c = jnp.maximum(col_reduce(s_c, jnp.max), 0.1 * NEG)
        e_c = jnp.exp2(s_c - m_c)
        oi = _dot(lc_ref[0, 0, :, 0:nc], e_c.astype(BF16))
        r_c = 1.0 / jnp.maximum(oi[d:d + 1], tiny)
        imp = oi[VROWS:] * r_c
        return oi[0:d] * r_c, sum(imp[:, r * QB:(r + 1) * QB] for r in range(R))

    if NC % 256 == 0:
        o_c, imp = lax.cond((NC // 2) * CMP_STRIDE + (CMP_LEN - 1) > q0 + QB - 1,
                            lambda: compressed(NC // 2), lambda: compressed(NC))
    else:
        o_c, imp = compressed(NC)

    blk_i = lax.broadcasted_iota(jnp.int32, (NB, QB), 0)
    blk = blk_i.astype(F32)
    t_b = q0 + lax.broadcasted_iota(jnp.int32, (NB, QB), 1)
    cur = t_b // SEL_LEN
    forced = (blk_i == 0) | (blk_i == cur) | (blk_i == cur - 1)
    mbias = jnp.where(forced, 0.0, NEG)
    score = jnp.where(forced, NEG, jnp.where(blk_i * SEL_LEN <= t_b, imp, NEG))
    for _ in range(min(SEL_TOPK, NB) - 3):
        mx = jnp.max(score, axis=0, keepdims=True)
        first = jnp.min(jnp.where(score == mx, blk, float(NB)), axis=0, keepdims=True)
        pick = blk == first
        mbias = jnp.where(pick, 0.0, mbias)
        score = jnp.where(pick, NEG, score)
    mb_ref[0:NB, :] = mbias
    if NB < mb_ref.shape[0]:
        mb_ref[NB:, :] = jnp.zeros((mb_ref.shape[0] - NB, QB), F32)

    def scores(j):
        k0 = pl.multiple_of(j * KT, KT)
        mb = mb_ref[pl.ds(pl.multiple_of(j * TB, TB), TB), :].astype(BF16)
        return _dot(ks_ref[0, pl.ds(k0, KT), :], rhs(qh_t, lanes4(mb)))

    def update(j, s, stats):
        m_old, acc = stats
        k0 = pl.multiple_of(j * KT, KT)
        m_new = jnp.maximum(m_old, col_reduce(s, jnp.max))
        alpha = jnp.exp2(m_old - m_new)
        pexp = jnp.exp2(s - m_new)
        return m_new, alpha * acc + _dot(vst_ref[0, :, pl.ds(k0, KT)], pexp.astype(BF16))

    def pair(p, stats):
        sb_ref[...] = scores(2 * p + 1)
        stats = update(2 * p, sa_ref[...], stats)
        sa_ref[...] = scores(2 * p + 2)
        return update(2 * p + 1, sb_ref[...], stats)

    def diag(j, s, stats):
        key = j * KT + lax.broadcasted_iota(jnp.int32, (KT, QB), 0)
        t_s = q0 + lax.broadcasted_iota(jnp.int32, (KT, QB), 1)
        return update(j, s + lanes4(jnp.where(key <= t_s, 0.0, NEG)), stats)

    n_full = q0 // KT
    init = (jnp.full((1, R * QB), NEG, F32), jnp.zeros((VROWS, R * QB), F32))
    sa_ref[...] = scores(0)
    stats = lax.fori_loop(0, n_full // 2, pair, init)
    j_even = (n_full // 2) * 2

    def odd_tail(stats):
        sb_ref[...] = scores(j_even + 1)
        stats = update(j_even, sa_ref[...], stats)
        return diag(j_even + 1, sb_ref[...], stats)

    _, acc_s = lax.cond(n_full % 2 == 1, odd_tail, lambda st: diag(j_even, sa_ref[...], st), stats)

    kwin = jnp.concatenate([kw0[0], kw1[0], kw2[0], kw3[0], kw4[0]], axis=0)
    vwin = jnp.concatenate([vw0[0], vw1[0], vw2[0], vw3[0], vw4[0]], axis=1)
    NW = WINDOW + QB
    kpos = q0 - WINDOW + lax.broadcasted_iota(jnp.int32, (NW, QB), 0)
    s_w = _dot(kwin, rhs_h) + lanes4(bw_ref[...] + jnp.where(kpos >= 0, 0.0, NEG))
    e_w = jnp.exp2(s_w - col_reduce(s_w, jnp.max))
    o_w = _dot(vwin, e_w.astype(BF16))

    def gate(c):
        return jnp.concatenate([g_ref[0, c, 0, r:r + 1, :] for r in range(R)], axis=1)

    o_t = (gate(0) * o_c + (gate(1) / acc_s[d:d + 1]) * acc_s[0:d]
           + (gate(2) / o_w[d:d + 1]) * o_w[0:d])
    o_ref[0] = jnp.concatenate(
        [jnp.concatenate([o_t[:, (2 * c) * QB:(2 * c + 1) * QB], o_t[:, (2 * c + 1) * QB:(2 * c + 2) * QB]],
                         axis=0).T for c in range(R // 2)], axis=1)


def nsa_attention_t(qh_t, ql_t, kch, kcl, vc_t, ks, vs_t, kw, vw_t, gates_t, batch, seq):
    W = qh_t.shape[0]
    G, R, d = NSA_GROUPS, NSA_REP, NSA_HEAD_DIM
    RW = R * d
    n_chunk = kch.shape[2]
    qpb = seq // Q_BLOCK
    wpb = WINDOW // Q_BLOCK
    n_cmp = n_chunk - CMP_LEN // CMP_STRIDE + 1
    n_sel = seq // SEL_LEN
    n = np.arange(n_chunk)[None, :]
    j = np.arange(n_sel)[:, None]
    overlap_t = ((n * CMP_STRIDE <= j * SEL_LEN + SEL_LEN - 1) & (n * CMP_STRIDE + CMP_LEN - 1 >= j * SEL_LEN)
                 & (n < n_cmp)).astype(np.float32)
    lc = jnp.concatenate([vc_t, jnp.broadcast_to(jnp.asarray(overlap_t, BF16), (batch, G, n_sel, n_chunk))],
                         axis=2)
    d = NSA_VROWS
    dist = np.arange(Q_BLOCK)[None, :] + WINDOW - np.arange(WINDOW + Q_BLOCK)[:, None]
    bias_w = np.where((dist >= 0) & (dist < WINDOW), 0.0, NEG).astype(np.float32)
    mb_rows = max(n_sel, SEL_TILE_BLOCKS)
    qblk = pl.BlockSpec((RW, Q_BLOCK), lambda b, g, i: (g, b * qpb + i))
    cblk = pl.BlockSpec((1, 1, n_chunk, LANES), lambda b, g, i: (b, g, 0, 0))
    wpos = lambda b, i, o: b * qpb + jnp.maximum(i + o - wpb, 0)
    kwblk = lambda o: pl.BlockSpec((1, Q_BLOCK, LANES), lambda b, g, i: (g, wpos(b, i, o), 0))
    vwblk = lambda o: pl.BlockSpec((1, d, Q_BLOCK), lambda b, g, i: (g, 0, wpos(b, i, o)))
    return pl.pallas_call(
        functools.partial(_nsa_attn_t_body, n_cmp=n_cmp),
        grid=(batch, G, qpb),
        in_specs=[qblk, qblk, cblk, cblk,
                  pl.BlockSpec((1, 1, d + n_sel, n_chunk), lambda b, g, i: (b, g, 0, 0)),
                  pl.BlockSpec((1, seq, LANES), lambda b, g, i: (g, b, 0)),
                  pl.BlockSpec((1, d, seq), lambda b, g, i: (g, 0, b))]
                 + [kwblk(o) for o in range(5)] + [vwblk(o) for o in range(5)]
                 + [pl.BlockSpec((1, 3, 1, R, Q_BLOCK), lambda b, g, i: (b, 0, g, 0, i)),
                    _full((WINDOW + Q_BLOCK, Q_BLOCK))],
        out_specs=pl.BlockSpec((1, Q_BLOCK, RW), lambda b, g, i: (b, i, g)),
        out_shape=jax.ShapeDtypeStruct((batch, seq, W), F32),
        scratch_shapes=[pltpu.VMEM((mb_rows, Q_BLOCK), F32)]
                       + [pltpu.VMEM((SEL_TILE_BLOCKS * SEL_LEN, R * Q_BLOCK), F32)] * 2,
        compiler_params=_cparams("parallel", "parallel", "arbitrary"),
        name="nsa_attention",
    )(qh_t, ql_t, kch, kcl, lc, ks, vs_t, *([kw] * 5), *([vw_t] * 5), gates_t, jnp.asarray(bias_w))


def nsa_mix_t(z_rest3, p, cos, sin):
    batch, seq, _ = z_rest3.shape
    G, R, d = NSA_GROUPS, NSA_REP, NSA_HEAD_DIM
    W = NSA_HEADS * d
    KW = G * d
    z2 = z_rest3.reshape(batch * seq, -1)
    qh_t, ql_t, ks, kw, vs_t, vw_t, gsig = nsa_prep(z2, p, seq, cos, sin)
    base = 2 * SEG + W
    raw = lambda i: z_rest3[..., base + i * KW:base + (i + 1) * KW]
    kch, kcl, vc = nsa_compress(raw(0), raw(1), p, batch, seq)
    n_chunk = vc.shape[2]
    ones = jnp.zeros((NSA_VROWS - d, n_chunk), BF16).at[0].set(1.0)
    vc_t = jnp.concatenate([vc[..., :d].transpose(0, 1, 3, 2),
                            jnp.broadcast_to(ones, (batch, G, NSA_VROWS - d, n_chunk))], axis=2)
    gates_t = (gsig[:, S_NSA:S_NSA + 3 * NSA_HEADS].reshape(batch, seq, G, R, 3).transpose(0, 4, 2, 3, 1))
    return nsa_attention_t(qh_t, ql_t, kch, kcl, vc_t, ks, vs_t, kw, vw_t, gates_t, batch, seq)


def _merge_body(h_ref, zg_ref, ya_ref, yb_ref, yc_ref, pa_ref, pb_ref, pc_ref, wo_ref, o_ref):
    D = D_MODEL

    def branch(i, y_ref, p_ref):
        gate = jax.nn.sigmoid(zg_ref[:, i * D:(i + 1) * D].astype(F32))
        return gate * _dot(y_ref[...].astype(BF16), p_ref[...])

    merged = branch(0, ya_ref, pa_ref) + branch(1, yb_ref, pb_ref) + branch(2, yc_ref, pc_ref)
    o_ref[...] = h_ref[...] + _dot(merged.astype(BF16), wo_ref[...])


def merge_out(h, z_gate, y_rw, y_gla, y_nsa, p, tm=256):
    T, D = h.shape
    W = y_rw.shape[1]
    row = lambda w: pl.BlockSpec((tm, w), lambda i: (i, 0))
    return pl.pallas_call(
        _merge_body,
        grid=(T // tm,),
        in_specs=[row(D), row(3 * D), row(W), row(W), row(W),
                  _full((W, D)), _full((W, D)), _full((W, D)), _full((D, D))],
        out_specs=row(D),
        out_shape=jax.ShapeDtypeStruct((T, D), F32),
        compiler_params=_cparams("parallel"),
        name="merge_out",
    )(h, z_gate, y_rw, y_gla, y_nsa, p['proj_a'].astype(BF16), p['proj_b'].astype(BF16),
      p['proj_c'].astype(BF16), p['w_out'].astype(BF16))


def _ffn_body(h_ref, g_ref, wg_ref, wu_ref, wd_ref, o_ref, xn_ref, acc_ref):
    f = pl.program_id(1)

    @pl.when(f == 0)
    def _():
        x = h_ref[...]
        ms = jnp.mean(x * x, axis=-1, keepdims=True)
        xn_ref[...] = (x * lax.rsqrt(ms + NORM_EPS) * g_ref[...]).astype(BF16)
        acc_ref[...] = jnp.zeros_like(acc_ref)

    xn = xn_ref[...]
    a = jax.nn.silu(_dot(xn, wg_ref[...])) * _dot(xn, wu_ref[...])
    acc_ref[...] += _dot(a.astype(BF16), wd_ref[...])

    @pl.when(f == pl.num_programs(1) - 1)
    def _():
        o_ref[...] = h_ref[...] + acc_ref[...]


def ffn_dense(h, p, tm=512, tf=512):
    T, D = h.shape
    F = p['ffn_gate'].shape[1]
    return pl.pallas_call(
        _ffn_body,
        grid=(T // tm, F // tf),
        in_specs=[pl.BlockSpec((tm, D), lambda i, f: (i, 0)), _full((1, D)),
                  pl.BlockSpec((D, tf), lambda i, f: (0, f)), pl.BlockSpec((D, tf), lambda i, f: (0, f)),
                  pl.BlockSpec((tf, D), lambda i, f: (f, 0))],
        out_specs=pl.BlockSpec((tm, D), lambda i, f: (i, 0)),
        out_shape=jax.ShapeDtypeStruct((T, D), F32),
        scratch_shapes=[pltpu.VMEM((tm, D), BF16), pltpu.VMEM((tm, D), F32)],
        compiler_params=_cparams("parallel", "arbitrary"),
        name="ffn_dense",
    )(h, p['norm_ffn'].reshape(1, D), p['ffn_gate'].astype(BF16), p['ffn_up'].astype(BF16),
      p['ffn_down'].astype(BF16))


def _router_body(h_ref, g_ref, wr_ref, xn_ref, route_ref):
    x = h_ref[...]
    ms = jnp.mean(x * x, axis=-1, keepdims=True)
    xn = x * lax.rsqrt(ms + NORM_EPS) * g_ref[...]
    xn_ref[...] = xn.astype(BF16)
    logits = _dot(xn, wr_ref[...], HI)
    lane = lax.broadcasted_iota(jnp.int32, logits.shape, 1)
    valid = lane < N_EXPERTS
    probs = _masked_softmax(logits, valid)
    p1 = jnp.max(probs, axis=-1, keepdims=True)
    i1 = jnp.min(jnp.where(valid & (probs == p1), lane, LANES), axis=-1, keepdims=True)
    rest = jnp.where(valid & (lane != i1), probs, -1.0)
    p2 = jnp.max(rest, axis=-1, keepdims=True)
    i2 = jnp.min(jnp.where(rest == p2, lane, LANES), axis=-1, keepdims=True)
    tot = p1 + p2
    route_ref[...] = jnp.where(lane == 0, i1.astype(F32),
                               jnp.where(lane == 1, i2.astype(F32),
                                         jnp.where(lane == 2, p1 / tot, jnp.where(lane == 3, p2 / tot, 0.0))))


def moe_router(h, p, tm=512):
    T, D = h.shape
    wr = jnp.zeros((D, LANES), F32).at[:, :N_EXPERTS].set(p['router'])
    return pl.pallas_call(
        _router_body,
        grid=(T // tm,),
        in_specs=[pl.BlockSpec((tm, D), lambda i: (i, 0)), _full((1, D)), _full((D, LANES))],
        out_specs=[pl.BlockSpec((tm, D), lambda i: (i, 0)), pl.BlockSpec((tm, LANES), lambda i: (i, 0))],
        out_shape=[jax.ShapeDtypeStruct((T, D), BF16), jax.ShapeDtypeStruct((T, LANES), F32)],
        compiler_params=_cparams("parallel"),
        name="moe_router",
    )(h, p['norm_ffn'].reshape(1, D), wr)


def _moe_body(te_ref, tv_ref, x_ref, wg_ref, wu_ref, wd_ref, o_ref, acc_ref):
    t = pl.program_id(0)
    f = pl.program_id(1)

    @pl.when(f == 0)
    def _():
        acc_ref[...] = jnp.zeros_like(acc_ref)

    @pl.when(tv_ref[t] > 0)
    def _():
        x = x_ref[...]
        a = jax.nn.silu(_dot(x, wg_ref[0])) * _dot(x, wu_ref[0])
        acc_ref[...] += _dot(a.astype(BF16), wd_ref[0])

    @pl.when(f == pl.num_programs(1) - 1)
    def _():
        o_ref[...] = acc_ref[...].astype(o_ref.dtype)


def moe_grouped(x_sorted, tile_expert, tile_valid, p, tm, tf=1024):
    P, D = x_sorted.shape
    F = p['exp_gate'].shape[2]
    nf = F // tf

    def fidx(f, tv, t):
        return jnp.where(tv[t] > 0, f, nf - 1)

    grid_spec = pltpu.PrefetchScalarGridSpec(
        num_scalar_prefetch=2,
        grid=(P // tm, nf),
        in_specs=[pl.BlockSpec((tm, D), lambda t, f, te, tv: (t, 0)),
                  pl.BlockSpec((1, D, tf), lambda t, f, te, tv: (te[t], 0, fidx(f, tv, t))),
                  pl.BlockSpec((1, D, tf), lambda t, f, te, tv: (te[t], 0, fidx(f, tv, t))),
                  pl.BlockSpec((1, tf, D), lambda t, f, te, tv: (te[t], fidx(f, tv, t), 0))],
        out_specs=pl.BlockSpec((tm, D), lambda t, f, te, tv: (t, 0)),
        scratch_shapes=[pltpu.VMEM((tm, D), F32)],
    )
    return pl.pallas_call(
        _moe_body,
        grid_spec=grid_spec,
        out_shape=jax.ShapeDtypeStruct((P, D), BF16),
        compiler_params=_cparams("arbitrary", "arbitrary"),
        name="moe_grouped",
    )(tile_expert, tile_valid, x_sorted, p['exp_gate'].astype(BF16), p['exp_up'].astype(BF16),
      p['exp_down'].astype(BF16))


def _combine_body(h_ref, ya_ref, yb_ref, route_ref, o_ref):
    w = route_ref[...]
    o_ref[...] = h_ref[...] + w[:, 2:3] * ya_ref[...].astype(F32) + w[:, 3:4] * yb_ref[...].astype(F32)


def moe_combine(h, ya, yb, route, tm=512):
    T, D = h.shape
    row = lambda w: pl.BlockSpec((tm, w), lambda i: (i, 0))
    return pl.pallas_call(
        _combine_body,
        grid=(T // tm,),
        in_specs=[row(D), row(D), row(D), row(LANES)],
        out_specs=row(D),
        out_shape=jax.ShapeDtypeStruct((T, D), F32),
        compiler_params=_cparams("parallel"),
        name="moe_combine",
    )(h, ya, yb, route)


def moe_ffn(h, p, tm=512):
    T, D = h.shape
    xn, route = moe_router(h, p)
    experts = route[:, 0:TOP_K].astype(jnp.int32).reshape(-1)
    n_pairs = T * TOP_K
    n_tiles = n_pairs // tm + N_EXPERTS
    eid = jnp.arange(N_EXPERTS, dtype=jnp.int32)
    onehot = (experts[:, None] == eid[None, :]).astype(jnp.int32)
    running = jnp.cumsum(onehot, axis=0)
    counts = running[-1]
    tiles_e = (counts + tm - 1) // tm
    tile_end = jnp.cumsum(tiles_e)
    row_start = (tile_end - tiles_e) * tm
    pair_start = jnp.cumsum(counts) - counts
    dest = jnp.sum(onehot * (row_start[None, :] + running - 1), axis=1)
    order = jnp.argsort(experts, stable=True).astype(jnp.int32)
    tiles = jnp.arange(n_tiles, dtype=jnp.int32)
    tile_valid = (tiles < tile_end[-1]).astype(jnp.int32)
    tile_expert = jnp.minimum(jnp.sum((tiles[:, None] >= tile_end[None, :]).astype(jnp.int32), axis=1),
                              N_EXPERTS - 1)
    last_e = jnp.minimum(jnp.sum((tile_end[-1] - 1 >= tile_end).astype(jnp.int32)), N_EXPERTS - 1)
    tile_expert = jnp.where(tile_valid > 0, tile_expert, last_e)
    row_oh = (jnp.repeat(tile_expert, tm)[:, None] == eid[None, :]).astype(jnp.int32)
    k_in_e = jnp.arange(n_tiles * tm, dtype=jnp.int32) - jnp.sum(row_oh * row_start[None, :], axis=1)
    row_ok = (k_in_e < jnp.sum(row_oh * counts[None, :], axis=1)) & (jnp.repeat(tile_valid, tm) > 0)
    src = jnp.clip(jnp.sum(row_oh * pair_start[None, :], axis=1) + k_in_e, 0, n_pairs - 1)
    row_token = jnp.where(row_ok, order[src] // TOP_K, 0)
    y_sorted = moe_grouped(xn[row_token], tile_expert, tile_valid, p, tm)
    dest = dest.reshape(T, TOP_K)
    return moe_combine(h, y_sorted[dest[:, 0]], y_sorted[dest[:, 1]], route)


def _reorder_w_in(w_in, has_vres):
    D = D_MODEL
    n_rw = 3 * RW_WIDTH + 160 + (32 if has_vres else 0)
    rw0 = 3 * D
    gla0 = rw0 + n_rw
    nsa0 = gla0 + 1552
    col = lambda a, b: w_in[:, a:b]
    zeros = lambda n: jnp.zeros((D, n), w_in.dtype)
    small = [col(rw0 + 1536, rw0 + n_rw)] + ([] if has_vres else [zeros(32)])
    small += [col(gla0 + 1024, gla0 + 1040), col(nsa0 + 1280, nsa0 + 1304), zeros(SMALL - S_NSA - 24)]
    rest = [col(rw0, rw0 + 1536), col(gla0, gla0 + 1024), col(gla0 + 1040, gla0 + 1552),
            col(nsa0, nsa0 + 1280)] + small
    return w_in[:, :rw0].astype(BF16), jnp.concatenate(rest, axis=1).astype(BF16)


def _layer(h, p, v_first, batch, seq, cos, sin, is_moe):
    T = batch * seq
    w_gate, w_rest = _reorder_w_in(p['w_in'], v_first is not None)
    z_gate = norm_matmul(h, p['norm_mix'], w_gate, BF16, 1024, 1536)
    z_rest = norm_matmul(h, p['norm_mix'], w_rest, F32, 1024, 1536)
    z_rest3 = z_rest.reshape(batch, seq, -1)
    prep = rwkv_prep(z_rest, v_first, p, seq)
    if v_first is None:
        v_first = prep[3]
    y_rw = rwkv_chunk(prep, p, batch, seq).reshape(T, -1)
    y_gla = gla_mix(z_rest3, p).reshape(T, -1)
    y_nsa = nsa_mix_t(z_rest3, p, cos, sin).reshape(T, -1)
    h = merge_out(h, z_gate, y_rw, y_gla, y_nsa, p)
    h = moe_ffn(h, p) if is_moe else ffn_dense(h, p)
    return h, v_first


_L0 = ('norm_mix', 'w_in', 'rw_mu', 'rw_w0', 'rw_w2', 'rw_a0', 'rw_a2', 'rw_g2', 'rw_kk', 'rw_ka', 'rw_rk',
       'rw_ln_w', 'rw_ln_b', 'gla_wg', 'gla_bg', 'gla_norm', 'nsa_qnorm', 'nsa_kcnorm', 'nsa_ksnorm',
       'nsa_kwnorm', 'nsa_pe_k', 'nsa_c1_k', 'nsa_c2_k', 'nsa_pe_v', 'nsa_c1_v', 'nsa_c2_v', 'proj_a', 'proj_b',
       'proj_c', 'w_out', 'norm_ffn', 'ffn_gate', 'ffn_up', 'ffn_down')
_L1 = ('norm_mix', 'w_in', 'rw_mu', 'rw_w0', 'rw_w2', 'rw_a0', 'rw_a2', 'rw_g2', 'rw_v0', 'rw_v2', 'rw_kk',
       'rw_ka', 'rw_rk', 'rw_ln_w', 'rw_ln_b', 'gla_wg', 'gla_bg', 'gla_norm', 'nsa_qnorm', 'nsa_kcnorm',
       'nsa_ksnorm', 'nsa_kwnorm', 'nsa_pe_k', 'nsa_c1_k', 'nsa_c2_k', 'nsa_pe_v', 'nsa_c1_v', 'nsa_c2_v',
       'proj_a', 'proj_b', 'proj_c', 'w_out', 'norm_ffn', 'router', 'exp_gate', 'exp_up', 'exp_down')


def kernel(x, *weights):
    p0 = dict(zip(_L0, weights[:len(_L0)]))
    p1 = dict(zip(_L1, weights[len(_L0):]))
    batch, seq, D = x.shape
    cos, sin = _rope_tables(jnp.arange(seq), NSA_HEADS * NSA_HEAD_DIM)
    h = x.reshape(batch * seq, D)
    h, v_first = _layer(h, p0, None, batch, seq, cos, sin, False)
    h, _ = _layer(h, p1, v_first, batch, seq, cos, sin, True)
    return h.reshape(batch, seq, D)
```

```python
import functools

import numpy as np
import jax
import jax.numpy as jnp
from jax import lax
from jax.experimental import pallas as pl
from jax.experimental.pallas import tpu as pltpu

F32 = jnp.float32
BF16 = jnp.bfloat16
HI = lax.Precision.HIGHEST

D_MODEL = 2048
RW_HEADS = 8
RW_HEAD_DIM = 64
RW_WIDTH = 512
RW_GN_EPS = 64e-5
GLA_HEADS = 4
GLA_DK = 64
GLA_DV = 128
GLA_TAU = 16.0
CHUNK = 64
NSA_HEADS = 8
NSA_GROUPS = 2
NSA_REP = NSA_HEADS // NSA_GROUPS
NSA_HEAD_DIM = 64
KV_REP = 2
SEL_TILE_BLOCKS = 16
NSA_VROWS = NSA_HEAD_DIM + 16
CMP_LEN = 32
CMP_STRIDE = 16
SEL_LEN = 64
SEL_TOPK = 16
WINDOW = 512
Q_BLOCK = 128
ROPE_THETA = 10000.0
N_EXPERTS = 8
TOP_K = 2
NORM_EPS = 1e-6

LANES = 128
VMEM_LIMIT = 56 * 1024 * 1024

SEG = 1536
SMALL = 256
SMALL_OFF = 3 * SEG - SMALL
S_ZW, S_ZA, S_ZG, S_ZV, S_GLA, S_NSA = 0, 32, 64, 160, 192, 208
NEG = -1e30
LOG2E = 1.4426950408889634


def _dot(a, b, precision=None):
    return jnp.dot(a, b, preferred_element_type=F32, precision=precision)


def _dot_nt(a, b, precision=None):
    return lax.dot_general(a, b, (((1,), (1,)), ((), ())), preferred_element_type=F32, precision=precision)


def _dot_tn(a, b, precision=None):
    return lax.dot_general(a, b, (((0,), (0,)), ((), ())), preferred_element_type=F32, precision=precision)


def _split_bf16(x):
    hi = x.astype(BF16)
    return hi, (x - hi.astype(F32)).astype(BF16)


def _dot_rhs_exact(a, b):
    hi, lo = _split_bf16(a)
    return _dot(hi, b) + _dot(lo, b)


def _dot_lhs_exact(a, b):
    hi, lo = _split_bf16(b)
    return _dot(a, hi) + _dot(a, lo)


def _dot_split(a, b_hi, b_lo):
    hi, lo = _split_bf16(a)
    return _dot(hi, b_hi) + _dot(hi, b_lo) + _dot(lo, b_hi)


def _cparams(*sem, flags=None):
    return pltpu.CompilerParams(dimension_semantics=sem, vmem_limit_bytes=VMEM_LIMIT, flags=flags)


def _full(shape):
    n = len(shape)
    return pl.BlockSpec(shape, lambda *_: (0,) * n)


def _norm_matmul_body(x_ref, g_ref, w_ref, o_ref, xn_ref):
    @pl.when(pl.program_id(1) == 0)
    def _():
        x = x_ref[...]
        ms = jnp.mean(x * x, axis=-1, keepdims=True)
        xn_ref[...] = (x * lax.rsqrt(ms + NORM_EPS) * g_ref[...]).astype(BF16)

    o_ref[...] = _dot(xn_ref[...], w_ref[...]).astype(o_ref.dtype)


def norm_matmul(x, g, w, out_dtype, tm, tn):
    M, K = x.shape
    N = w.shape[1]
    return pl.pallas_call(
        _norm_matmul_body,
        grid=(M // tm, N // tn),
        in_specs=[pl.BlockSpec((tm, K), lambda i, j: (i, 0)),
                  pl.BlockSpec((1, K), lambda i, j: (0, 0)),
                  pl.BlockSpec((K, tn), lambda i, j: (0, j))],
        out_specs=pl.BlockSpec((tm, tn), lambda i, j: (i, j)),
        out_shape=jax.ShapeDtypeStruct((M, N), out_dtype),
        scratch_shapes=[pltpu.VMEM((tm, K), BF16)],
        compiler_params=_cparams("parallel", "arbitrary"),
        name="norm_matmul",
    )(x, g.reshape(1, K), w)


def _shift_rows(x, prev_row):
    rolled = pltpu.roll(x, 1, axis=0)
    first = lax.broadcasted_iota(jnp.int32, x.shape, 0) == 0
    return jnp.where(first, prev_row, rolled)


def _rwkv_prep_body(*refs, tiles_per_seq, has_vres):
    if has_vres:
        (zm_ref, zs_ref, vf_ref, mum_ref, mus_ref, vec_ref, w2_ref, a2_ref, g2_ref, v2_ref, bd_ref,
         r_ref, lw_ref, k_ref, v_ref, kk_ref, b_ref, g_ref, bonus_ref, pm_ref, ps_ref) = refs
    else:
        (zm_ref, zs_ref, mum_ref, mus_ref, vec_ref, w2_ref, a2_ref, g2_ref, bd_ref,
         r_ref, lw_ref, k_ref, v_ref, kk_ref, b_ref, g_ref, bonus_ref, pm_ref, ps_ref) = refs
    i = pl.program_id(0)

    @pl.when(i % tiles_per_seq == 0)
    def _():
        pm_ref[...] = jnp.zeros_like(pm_ref)
        ps_ref[...] = jnp.zeros_like(ps_ref)

    zm = zm_ref[...]
    zs = zs_ref[...]
    tm = zm.shape[0]
    zm_prev = _shift_rows(zm, pm_ref[0:1, :])
    zs_prev = _shift_rows(zs, ps_ref[0:1, :])
    pm_ref[0:1, :] = zm[tm - 1:tm, :]
    ps_ref[0:1, :] = zs[tm - 1:tm, :]
    zm = zm + (zm_prev - zm) * mum_ref[...]
    zs = zs + (zs_prev - zs) * mus_ref[...]

    r = zm[:, 0:RW_WIDTH]
    k = zm[:, RW_WIDTH:2 * RW_WIDTH]
    v = zm[:, 2 * RW_WIDTH:3 * RW_WIDTH]
    w0 = vec_ref[0:1, :]
    a0 = vec_ref[1:2, :]
    v0 = vec_ref[2:3, :]
    kkw = vec_ref[3:4, :]
    ka = vec_ref[4:5, :]
    rk = vec_ref[5:6, :]

    w_log = -jax.nn.softplus(-(w0 + _dot_split(jnp.tanh(zs), w2_ref[0], w2_ref[1]))) - 0.5
    lw = -jnp.exp(w_log)
    a = jax.nn.sigmoid(a0 + _dot_split(zs, a2_ref[0], a2_ref[1]))
    g = _dot_split(jax.nn.sigmoid(zs), g2_ref[0], g2_ref[1])
    if has_vres:
        v = v + (vf_ref[...] - v) * jax.nn.sigmoid(v0 + _dot_split(zs, v2_ref[0], v2_ref[1]))

    bd = bd_ref[...]
    kk = k * kkw
    nrm = jnp.sqrt(_dot_rhs_exact(kk * kk, bd))
    kk = kk / jnp.maximum(nrm, 1e-12)
    k2 = k * (1.0 + (a - 1.0) * ka)
    bonus = _dot_rhs_exact(r * k2 * rk, bd) * v

    r_ref[...] = r
    lw_ref[...] = lw
    k_ref[...] = k2
    v_ref[...] = v
    kk_ref[...] = kk
    b_ref[...] = kk * a
    g_ref[...] = g
    bonus_ref[...] = bonus


def _pad_rows(w, off, total=SMALL):
    hi, lo = _split_bf16(jnp.zeros((total, w.shape[1]), F32).at[off:off + w.shape[0]].set(w))
    return jnp.stack([hi, lo])


def rwkv_prep(z_rest, v_first, p, seq, tm=256):
    T = z_rest.shape[0]
    has_vres = v_first is not None
    mu = p['rw_mu']
    n_small = mu.shape[0] - 3 * RW_WIDTH
    mum = mu[:3 * RW_WIDTH].reshape(1, -1)
    mus = jnp.zeros((SMALL,), F32).at[:n_small].set(mu[3 * RW_WIDTH:]).reshape(1, -1)
    zero = jnp.zeros((RW_WIDTH,), F32)
    vec = jnp.stack([p['rw_w0'], p['rw_a0'], p['rw_v0'] if has_vres else zero, p['rw_kk'], p['rw_ka'],
                     p['rw_rk'].reshape(-1), zero, zero])
    w2 = _pad_rows(p['rw_w2'], S_ZW)
    a2 = _pad_rows(p['rw_a2'], S_ZA)
    g2 = _pad_rows(p['rw_g2'], S_ZG)
    hid = np.arange(RW_WIDTH) // RW_HEAD_DIM
    bd = jnp.asarray((hid[:, None] == hid[None, :]).astype(np.float32), BF16)
    row = lambda w: pl.BlockSpec((tm, w), lambda i: (i, 0))
    lora = _full((2, SMALL, RW_WIDTH))
    in_specs = [row(SEG), pl.BlockSpec((tm, SMALL), lambda i: (i, SMALL_OFF // SMALL))]
    args = [z_rest, z_rest]
    if has_vres:
        in_specs.append(row(RW_WIDTH))
        args.append(v_first)
    in_specs += [_full((1, SEG)), _full((1, SMALL)), _full((8, RW_WIDTH)), lora, lora, lora]
    args += [mum, mus, vec, w2, a2, g2]
    if has_vres:
        in_specs.append(lora)
        args.append(_pad_rows(p['rw_v2'], S_ZV))
    in_specs.append(_full((RW_WIDTH, RW_WIDTH)))
    args.append(bd)
    out = jax.ShapeDtypeStruct((T, RW_WIDTH), F32)
    return pl.pallas_call(
        functools.partial(_rwkv_prep_body, tiles_per_seq=seq // tm, has_vres=has_vres),
        grid=(T // tm,),
        in_specs=in_specs,
        out_specs=[row(RW_WIDTH)] * 8,
        out_shape=[out] * 8,
        scratch_shapes=[pltpu.VMEM((8, SEG), F32), pltpu.VMEM((8, SMALL), F32)],
        compiler_params=_cparams("arbitrary"),
        name="rwkv_prep",
    )(*args)


def _rwkv_chunk_body(r_ref, lw_ref, k_ref, v_ref, kk_ref, b_ref, g_ref, bonus_ref, lnw_ref, lnb_ref,
                     o_ref, state_ref, *, n_chunks):
    C = CHUNK
    N = RW_HEAD_DIM

    @pl.when(pl.program_id(1) == 0)
    def _():
        state_ref[...] = jnp.zeros_like(state_ref)

    ti = lax.broadcasted_iota(jnp.int32, (C, C), 0)
    si = lax.broadcasted_iota(jnp.int32, (C, C), 1)
    incl = si <= ti
    strict = si < ti
    tril1 = incl.astype(BF16)
    eye = (si == ti).astype(F32)
    masks = []
    m = 1
    while m < C:
        masks.append(((ti // (2 * m)) == (si // (2 * m))) & ((ti // m) % 2 == 1) & ((si // m) % 2 == 0))
        m *= 2
    lnw = lnw_ref[...]
    lnb = lnb_ref[...]

    NCH = range(n_chunks)
    H = range(RW_HEADS)
    U = [(c, h) for c in NCH for h in H]
    sl = [slice(h * N, (h + 1) * N) for h in H]
    ew = []
    for c in NCH:
        rows = pl.ds(c * C, C)
        lw = lw_ref[0, rows, :]
        k = k_ref[0, rows, :]
        b = b_ref[0, rows, :]
        cum = _dot_lhs_exact(tril1, lw)
        cum_last = cum[C - 1:C, :]
        e_neg = jnp.exp(-cum)
        e_rel = jnp.exp(cum_last - cum)
        ew.append(dict(
            rows=rows,
            alpha=(kk_ref[0, rows, :] * jnp.exp(cum - lw)).astype(BF16),
            beta=(b * e_neg).astype(BF16), kappa=(k * e_neg).astype(BF16),
            rho=(r_ref[0, rows, :] * jnp.exp(cum)).astype(BF16),
            beta2=(b * e_rel).astype(BF16), kappa2=(k * e_rel).astype(BF16),
            pc=jnp.exp(cum_last), v=v_ref[0, rows, :].astype(BF16)))
    A = [ew[c]['alpha'][:, sl[h]] for c, h in U]
    Rh = [ew[c]['rho'][:, sl[h]] for c, h in U]
    V = [ew[c]['v'][:, sl[h]] for c, h in U]
    p4 = [_dot_nt(jnp.concatenate([A[u], Rh[u]], axis=0),
                  jnp.concatenate([ew[c]['beta'][:, sl[h]], ew[c]['kappa'][:, sl[h]]], axis=0))
          for u, (c, h) in enumerate(U)]
    L = [jnp.where(strict, p[:C, :C], 0.0) for p in p4]
    T = [eye for _ in U]
    for mk in masks:
        n = [_dot(T[u].astype(BF16), jnp.where(mk, L[u], 0.0).astype(BF16)) for u in range(len(U))]
        T = [T[u] - _dot(n[u].astype(BF16), T[u].astype(BF16)) for u in range(len(U))]
    T = [t.astype(BF16) for t in T]
    kv = [_dot(jnp.concatenate([jnp.where(strict, p[:C, C:], 0.0), jnp.where(incl, p[C:, C:], 0.0)],
                               axis=0).astype(BF16), V[u]) for u, p in enumerate(p4)]
    lhs = [jnp.concatenate([_dot(T[u], A[u]).astype(BF16), Rh[u]], axis=0) for u in range(len(U))]
    W = [_dot(T[u], kv[u][:C].astype(BF16)) for u in range(len(U))]
    D = [_dot_tn(V[u], ew[c]['kappa2'][:, sl[h]]) for u, (c, h) in enumerate(U)]
    G = [jnp.where(incl, p[C:, :C], 0.0).astype(BF16) for p in p4]

    S = [state_ref[h] for h in H]
    for c in NCH:
        u0 = c * RW_HEADS
        rs = [_dot_nt(lhs[u0 + h], S[h].astype(BF16)) for h in H]
        Ub = [(-(rs[h][:C] + W[u0 + h])).astype(BF16) for h in H]
        Y = [rs[h][C:] + _dot(G[u0 + h], Ub[h]) + kv[u0 + h][C:] for h in H]
        S = [S[h] * ew[c]['pc'][:, sl[h]] + _dot_tn(Ub[h], ew[c]['beta2'][:, sl[h]]) + D[u0 + h] for h in H]
        yn = []
        for h in H:
            mu = jnp.mean(Y[h], axis=-1, keepdims=True)
            yc = Y[h] - mu
            var = jnp.mean(yc * yc, axis=-1, keepdims=True)
            yn.append(yc * lax.rsqrt(var + RW_GN_EPS))
        rows = ew[c]['rows']
        o_ref[0, rows, :] = ((jnp.concatenate(yn, axis=1) * lnw + lnb + bonus_ref[0, rows, :])
                             * g_ref[0, rows, :])
    state_ref[...] = jnp.stack(S)


def rwkv_chunk(prep, p, batch, seq, rows=256):
    r, lw, k2, v, kk, b, g, bonus = [t.reshape(batch, seq, RW_WIDTH) for t in prep]
    blk = pl.BlockSpec((1, rows, RW_WIDTH), lambda bi, i: (bi, i, 0))
    return pl.pallas_call(
        functools.partial(_rwkv_chunk_body, n_chunks=rows // CHUNK),
        grid=(batch, seq // rows),
        in_specs=[blk] * 8 + [_full((1, RW_WIDTH))] * 2,
        out_specs=blk,
        out_shape=jax.ShapeDtypeStruct((batch, seq, RW_WIDTH), F32),
        scratch_shapes=[pltpu.VMEM((RW_HEADS, RW_HEAD_DIM, RW_HEAD_DIM), F32)],
        compiler_params=_cparams("parallel", "arbitrary"),
        name="rwkv_chunk",
    )(r, lw, k2, v, kk, b, g, bonus, p['rw_ln_w'].reshape(1, -1), p['rw_ln_b'].reshape(1, -1))


def _gla_body(z_ref, zs_ref, wg_ref, bg_ref, gn_ref, o_ref, state_ref, *, n_chunks):
    C = CHUNK
    HK = GLA_HEADS * GLA_DK
    HV = GLA_HEADS * GLA_DV

    @pl.when(pl.program_id(1) == 0)
    def _():
        state_ref[...] = jnp.zeros_like(state_ref)

    RB = n_chunks * C
    ti = lax.broadcasted_iota(jnp.int32, (RB, RB), 0)
    si = lax.broadcasted_iota(jnp.int32, (RB, RB), 1)
    tril_blk = ((si <= ti) & (si // C == ti // C)).astype(BF16)
    causal = (lax.broadcasted_iota(jnp.int32, (C, HK), 1) % GLA_DK) <= lax.broadcasted_iota(jnp.int32, (C, HK), 0)
    blk_k = (lax.broadcasted_iota(jnp.int32, (HK, HK), 0) // GLA_DK) == (lax.broadcasted_iota(jnp.int32, (HK, HK), 1) // GLA_DK)
    blk_v = (lax.broadcasted_iota(jnp.int32, (HK, HV), 0) // GLA_DK) == (lax.broadcasted_iota(jnp.int32, (HK, HV), 1) // GLA_DV)
    blk_vt = (lax.broadcasted_iota(jnp.int32, (HV, HK), 0) // GLA_DV) == (lax.broadcasted_iota(jnp.int32, (HV, HK), 1) // GLA_DK)

    z = z_ref[0]
    log_a = jax.nn.log_sigmoid(_dot_split(zs_ref[0], wg_ref[0], wg_ref[1]) + bg_ref[...]) / GLA_TAU
    bcum_all = _dot_lhs_exact(tril_blk, log_a)
    pre = []
    for c in range(n_chunks):
        rows = slice(c * C, (c + 1) * C)
        q = z[rows, 0:HK] * (GLA_DK ** -0.5)
        k = z[rows, HK:2 * HK]
        vb = z[rows, 2 * HK:2 * HK + HV].astype(BF16)
        bcum = bcum_all[rows]
        b_last = bcum[C - 1:C, :]
        q_dec = (q * jnp.exp(bcum)).astype(BF16)
        k_inv = (k * jnp.exp(-bcum)).astype(BF16)
        k_dec = (k * jnp.exp(b_last - bcum)).astype(BF16)
        kbd = jnp.where(blk_k, jnp.concatenate([k_inv] * GLA_HEADS, axis=0), jnp.zeros((), BF16))
        att = jnp.where(causal, _dot_nt(q_dec, kbd), 0.0)
        vbd = jnp.where(blk_v, jnp.concatenate([vb] * GLA_HEADS, axis=0), jnp.zeros((), BF16))
        pre.append((q_dec, _dot(att.astype(BF16), vbd), jnp.exp(b_last),
                    jnp.where(blk_vt, _dot_tn(vb, k_dec), 0.0)))
    st = state_ref[...]
    outs = []
    for q_dec, o_intra, decay, inc in pre:
        outs.append(o_intra + _dot_nt(q_dec, st.astype(BF16)))
        st = st * decay + inc
    state_ref[...] = st
    o = jnp.concatenate(outs, axis=0)
    parts = []
    for h in range(GLA_HEADS):
        oh = o[:, h * GLA_DV:(h + 1) * GLA_DV]
        ms = jnp.mean(oh * oh, axis=-1, keepdims=True)
        parts.append(oh * lax.rsqrt(ms + NORM_EPS))
    on = jnp.concatenate(parts, axis=-1) * gn_ref[...]
    o_ref[0] = on * jax.nn.silu(z[:, 2 * HK + HV:2 * HK + 2 * HV])


def gla_mix(z_rest3, p, rows=256):
    batch, seq, _ = z_rest3.shape
    HV = GLA_HEADS * GLA_DV
    wg = _pad_rows(p['gla_wg'], S_GLA)
    return pl.pallas_call(
        functools.partial(_gla_body, n_chunks=rows // CHUNK),
        grid=(batch, seq // rows),
        in_specs=[pl.BlockSpec((1, rows, SEG), lambda b, i: (b, i, 1)),
                  pl.BlockSpec((1, rows, SMALL), lambda b, i: (b, i, SMALL_OFF // SMALL)),
                  _full((2, SMALL, GLA_HEADS * GLA_DK)), _full((1, GLA_HEADS * GLA_DK)), _full((1, HV))],
        out_specs=pl.BlockSpec((1, rows, HV), lambda b, i: (b, i, 0)),
        out_shape=jax.ShapeDtypeStruct((batch, seq, HV), F32),
        scratch_shapes=[pltpu.VMEM((HV, GLA_HEADS * GLA_DK), F32)],
        compiler_params=_cparams("parallel", "arbitrary"),
        name="gla_mix",
    )(z_rest3, z_rest3, wg, p['gla_bg'].reshape(1, -1), jnp.tile(p['gla_norm'], GLA_HEADS).reshape(1, -1))


def _rope(x, cos, sin_signed):
    w = x.shape[-1]
    lane = lax.broadcasted_iota(jnp.int32, x.shape, 1) % NSA_HEAD_DIM
    partner = jnp.where(lane < NSA_HEAD_DIM // 2, pltpu.roll(x, w - NSA_HEAD_DIM // 2, axis=1),
                        pltpu.roll(x, NSA_HEAD_DIM // 2, axis=1))
    return x * cos + partner * sin_signed


def _nsa_prep_body(z_ref, cos_ref, sin_ref, qn_ref, kn_ref, bd_ref,
                   qh_ref, ql_ref, ks_ref, kw_ref, vs_ref, vw_ref, g_ref, *, tiles_per_seq):
    z = z_ref[...]
    cos = cos_ref[...]
    sin = sin_ref[...]
    bd = bd_ref[...]
    d = NSA_HEAD_DIM
    W = NSA_HEADS * d
    KW = NSA_GROUPS * d
    tm = z.shape[0]

    def head_norm(x, g, n):
        ms = _dot_rhs_exact(x * x, bd[:n, :n]) * (1.0 / d)
        return x * lax.rsqrt(ms + NORM_EPS) * g

    q = head_norm(z[:, 0:W], qn_ref[...], W)
    qh_ref[...], ql_ref[...] = _split_bf16((_rope(q, cos, sin) * (d ** -0.5 * LOG2E)).T)

    lane = lax.broadcasted_iota(jnp.int32, (tm, LANES), 1)
    pos = (pl.program_id(0) % tiles_per_seq) * tm + lax.broadcasted_iota(jnp.int32, (tm, LANES), 0)
    onehot = jnp.where(lane - d == (pos // SEL_LEN) % SEL_TILE_BLOCKS, 1.0, 0.0)

    def keys(k, extra, ref):
        for g in range(NSA_GROUPS):
            kg = k if g == 0 else pltpu.roll(k, LANES - g * d, axis=1)
            ref[g] = jnp.where(lane < d, kg, extra).astype(BF16)

    def values_t(v, ref):
        vt = v.T
        tail = jnp.where(lax.broadcasted_iota(jnp.int32, (NSA_VROWS - d, tm), 0) == 0, 1.0, 0.0)
        for g in range(NSA_GROUPS):
            ref[g] = jnp.concatenate([vt[g * d:(g + 1) * d], tail], axis=0).astype(BF16)

    ks = head_norm(z[:, W + 2 * KW:W + 3 * KW], kn_ref[0:1, :], KW)
    keys(_rope(ks, cos[:, :KW], sin[:, :KW]), onehot, ks_ref)
    kw = head_norm(z[:, W + 4 * KW:W + 5 * KW], kn_ref[1:2, :], KW)
    keys(_rope(kw, cos[:, :KW], sin[:, :KW]), jnp.zeros((tm, LANES), F32), kw_ref)
    values_t(z[:, W + 3 * KW:W + 4 * KW], vs_ref)
    values_t(z[:, W + 5 * KW:W + 6 * KW], vw_ref)
    g_ref[...] = jax.nn.sigmoid(z[:, SEG - SMALL:SEG])


def _rope_tables(pos, width):
    inv = 1.0 / (ROPE_THETA ** (jnp.arange(0, NSA_HEAD_DIM, 2, dtype=F32) / NSA_HEAD_DIM))
    ang = pos.astype(F32)[:, None] * inv[None, :]
    c, s = jnp.cos(ang), jnp.sin(ang)
    reps = width // NSA_HEAD_DIM
    return jnp.tile(jnp.concatenate([c, c], -1), (1, reps)), jnp.tile(jnp.concatenate([-s, s], -1), (1, reps))


def nsa_prep(z_rest, p, seq, cos, sin, tm=256):
    T = z_rest.shape[0]
    W = NSA_HEADS * NSA_HEAD_DIM
    KW = NSA_GROUPS * NSA_HEAD_DIM
    hid = np.arange(W) // NSA_HEAD_DIM
    bd = jnp.asarray((hid[:, None] == hid[None, :]).astype(np.float32), BF16)
    qn = jnp.tile(p['nsa_qnorm'], NSA_HEADS).reshape(1, W)
    kn = jnp.stack([jnp.tile(p['nsa_ksnorm'], NSA_GROUPS), jnp.tile(p['nsa_kwnorm'], NSA_GROUPS)])
    tps = seq // tm
    tab = pl.BlockSpec((tm, W), lambda i: (i % tps, 0))
    G = NSA_GROUPS
    q_spec = pl.BlockSpec((W, tm), lambda i: (0, i))
    k_spec = pl.BlockSpec((G, tm, LANES), lambda i: (0, i, 0))
    v_spec = pl.BlockSpec((G, NSA_VROWS, tm), lambda i: (0, 0, i))
    q_shape = jax.ShapeDtypeStruct((W, T), BF16)
    k_shape = jax.ShapeDtypeStruct((G, T, LANES), BF16)
    v_shape = jax.ShapeDtypeStruct((G, NSA_VROWS, T), BF16)
    return pl.pallas_call(
        functools.partial(_nsa_prep_body, tiles_per_seq=tps),
        grid=(T // tm,),
        in_specs=[pl.BlockSpec((tm, SEG), lambda i: (i, 2)), tab, tab,
                  _full((1, W)), _full((2, KW)), _full((W, W))],
        out_specs=[q_spec, q_spec, k_spec, k_spec, v_spec, v_spec, pl.BlockSpec((tm, SMALL), lambda i: (i, 0))],
        out_shape=[q_shape, q_shape, k_shape, k_shape, v_shape, v_shape, jax.ShapeDtypeStruct((T, SMALL), F32)],
        compiler_params=_cparams("parallel"),
        name="nsa_prep",
    )(z_rest, cos, sin, qn, kn, bd)


def _nsa_compress_body(xk_ref, xv_ref, pek_ref, pev_ref, w1k_ref, w1v_ref, w2k_ref, w2v_ref, kn_ref,
                       cos_ref, sin_ref, kch_ref, kcl_ref, vc_ref):
    def compress(x, pe_ref, w1_ref, w2_ref):
        a = _dot((x + pe_ref[0:1, :]).astype(BF16), w1_ref[0])
        b = _dot((x + pe_ref[1:2, :]).astype(BF16), w1_ref[1])
        n = a.shape[0]
        h = jax.nn.gelu(a + pltpu.roll(b, n - 1, axis=0))
        return _dot(h.astype(BF16), w2_ref[...])

    kc = compress(xk_ref[0, 0], pek_ref, w1k_ref, w2k_ref)
    ms = jnp.mean(kc * kc, axis=-1, keepdims=True)
    kc = kc * lax.rsqrt(ms + NORM_EPS) * kn_ref[...]
    kc = kc * cos_ref[...] + pltpu.roll(kc, NSA_HEAD_DIM // 2, axis=1) * sin_ref[...]
    kch_ref[0, 0], kcl_ref[0, 0] = _split_bf16(kc)
    vc_ref[0, 0] = compress(xv_ref[0, 0], pev_ref, w1v_ref, w2v_ref).astype(BF16)


def nsa_compress(kc_raw, vc_raw, p, batch, seq):
    G, d = NSA_GROUPS, NSA_HEAD_DIM
    n_chunk = seq // CMP_STRIDE
    RW = KV_REP * d

    def chunks(t):
        t = t.reshape(batch, n_chunk, CMP_STRIDE, G, d).transpose(0, 3, 1, 2, 4)
        return t.reshape(batch, G, n_chunk, CMP_STRIDE * d)

    def w1(t):
        return t.reshape(2, CMP_STRIDE * d, d).astype(BF16)

    def pe(t):
        return t.reshape(2, CMP_STRIDE * d)

    c_end = jnp.arange(n_chunk) * CMP_STRIDE + (CMP_LEN - 1)
    cos_c, sin_c = _rope_tables(c_end, RW)
    blk = pl.BlockSpec((1, 1, n_chunk, CMP_STRIDE * d), lambda b, g: (b, g, 0, 0))
    oblk = pl.BlockSpec((1, 1, n_chunk, RW), lambda b, g: (b, g, 0, 0))
    return pl.pallas_call(
        _nsa_compress_body,
        grid=(batch, G),
        in_specs=[blk, blk, _full((2, CMP_STRIDE * d)), _full((2, CMP_STRIDE * d)),
                  _full((2, CMP_STRIDE * d, d)), _full((2, CMP_STRIDE * d, d)),
                  _full((d, RW)), _full((d, RW)), _full((1, RW)), _full((n_chunk, RW)), _full((n_chunk, RW))],
        out_specs=[oblk, oblk, oblk],
        out_shape=[jax.ShapeDtypeStruct((batch, G, n_chunk, RW), BF16)] * 3,
        compiler_params=_cparams("parallel", "parallel"),
        name="nsa_compress",
    )(chunks(kc_raw), chunks(vc_raw), pe(p['nsa_pe_k']), pe(p['nsa_pe_v']), w1(p['nsa_c1_k']), w1(p['nsa_c1_v']),
      jnp.tile(p['nsa_c2_k'], (1, KV_REP)).astype(BF16), jnp.tile(p['nsa_c2_v'], (1, KV_REP)).astype(BF16),
      jnp.tile(p['nsa_kcnorm'], KV_REP).reshape(1, RW), cos_c, sin_c)


def _masked_softmax(s, valid):
    m = jnp.max(jnp.where(valid, s, NEG), axis=-1, keepdims=True)
    e = jnp.where(valid, jnp.exp(s - m), 0.0)
    return e / jnp.maximum(jnp.sum(e, axis=-1, keepdims=True), jnp.finfo(F32).tiny)


def _nsa_attn_t_body(qh_ref, ql_ref, kch_ref, kcl_ref, lc_ref, ks_ref, vst_ref,
                     kw0, kw1, kw2, kw3, kw4, vw0, vw1, vw2, vw3, vw4,
                     g_ref, bw_ref, o_ref, mb_ref, sa_ref, sb_ref, *, n_cmp):
    QB = Q_BLOCK
    R = NSA_REP
    d = NSA_HEAD_DIM
    TB = SEL_TILE_BLOCKS
    KT = TB * SEL_LEN
    VROWS = NSA_VROWS
    q0 = pl.program_id(2) * QB
    tiny = jnp.finfo(F32).tiny

    def lanes4(x):
        return jnp.concatenate([x] * R, axis=1)

    def col_reduce(x, op):
        return op(x, axis=0, keepdims=True)

    def rhs(q_t, extra=None):
        parts = [jnp.concatenate([q_t[r * d:(r + 1) * d, :] for r in range(R)], axis=1)]
        if extra is not None:
            parts.append(extra)
        n = sum(x.shape[0] for x in parts)
        return jnp.concatenate(parts + [jnp.zeros((LANES - n, R * QB), BF16)], axis=0)

    qh_t = qh_ref[...]
    rhs_h = rhs(qh_t)
    rhs_l = rhs(ql_ref[...])

    NC = kch_ref.shape[2]
    NB = lc_ref.shape[2] - VROWS

    def compressed(nc):
        kch = kch_ref[0, 0, 0:nc, :]
        n_c = lax.broadcasted_iota(jnp.int32, (nc, QB), 0)
        t_c = q0 + lax.broadcasted_iota(jnp.int32, (nc, QB), 1)
        bias_c = jnp.where(n_c * CMP_STRIDE + (CMP_LEN - 1) <= t_c, jnp.where(n_c < n_cmp, 0.0, NEG), NEG)
        s_c = _dot(kch, rhs_h) + _dot(kcl_ref[0, 0, 0:nc, :], rhs_h) + _dot(kch, rhs_l) + lanes4(bias_c)
        m_c = jnp.maximum(col_reduce(s_c, jnp.max), 0.1 * NEG)
        e_c = jnp.exp2(s_c - m_c)
        oi = _dot(lc_ref[0, 0, :, 0:nc], e_c.astype(BF16))
        r_c = 1.0 / jnp.maximum(oi[d:d + 1], tiny)
        imp = oi[VROWS:] * r_c
        return oi[0:d] * r_c, sum(imp[:, r * QB:(r + 1) * QB] for r in range(R))

    if NC % 512 == 0:
        quarters = [NC * k // 4 for k in (1, 2, 3)]
        last_q = q0 + QB - 1
        needed = sum((nq * CMP_STRIDE + (CMP_LEN - 1) <= last_q).astype(jnp.int32) for nq in quarters)
        o_c, imp = lax.switch(needed, [functools.partial(compressed, nq) for nq in quarters + [NC]])
    else:
        o_c, imp = compressed(NC)

    blk_i = lax.broadcasted_iota(jnp.int32, (NB, QB), 0)
    blk = blk_i.astype(F32)
    t_b = q0 + lax.broadcasted_iota(jnp.int32, (NB, QB), 1)
    cur = t_b // SEL_LEN
    forced = (blk_i == 0) | (blk_i == cur) | (blk_i == cur - 1)
    mbias = jnp.where(forced, 0.0, NEG)
    score = jnp.where(forced, NEG, jnp.where(blk_i * SEL_LEN <= t_b, imp, NEG))
    for _ in range(min(SEL_TOPK, NB) - 3):
        mx = jnp.max(score, axis=0, keepdims=True)
        first = jnp.min(jnp.where(score == mx, blk, float(NB)), axis=0, keepdims=True)
        pick = blk == first
        mbias = jnp.where(pick, 0.0, mbias)
        score = jnp.where(pick, NEG, score)
    mb_ref[0:NB, :] = mbias
    if NB < mb_ref.shape[0]:
        mb_ref[NB:, :] = jnp.zeros((mb_ref.shape[0] - NB, QB), F32)

    def scores(j):
        k0 = pl.multiple_of(j * KT, KT)
        mb = mb_ref[pl.ds(pl.multiple_of(j * TB, TB), TB), :].astype(BF16)
        return _dot(ks_ref[0, pl.ds(k0, KT), :], rhs(qh_t, lanes4(mb)))

    def update(j, s, stats):
        m_old, acc = stats
        k0 = pl.multiple_of(j * KT, KT)
        m_new = jnp.maximum(m_old, col_reduce(s, jnp.max))
        alpha = jnp.exp2(m_old - m_new)
        pexp = jnp.exp2(s - m_new)
        return m_new, alpha * acc + _dot(vst_ref[0, :, pl.ds(k0, KT)], pexp.astype(BF16))

    def pair(p, stats):
        sb_ref[...] = scores(2 * p + 1)
        stats = update(2 * p, sa_ref[...], stats)
        sa_ref[...] = scores(2 * p + 2)
        return update(2 * p + 1, sb_ref[...], stats)

    def diag(j, s, stats):
        key = j * KT + lax.broadcasted_iota(jnp.int32, (KT, QB), 0)
        t_s = q0 + lax.broadcasted_iota(jnp.int32, (KT, QB), 1)
        return update(j, s + lanes4(jnp.where(key <= t_s, 0.0, NEG)), stats)

    n_full = q0 // KT
    init = (jnp.full((1, R * QB), NEG, F32), jnp.zeros((VROWS, R * QB), F32))
    sa_ref[...] = scores(0)
    stats = lax.fori_loop(0, n_full // 2, pair, init)
    j_even = (n_full // 2) * 2

    def odd_tail(stats):
        sb_ref[...] = scores(j_even + 1)
        stats = update(j_even, sa_ref[...], stats)
        return diag(j_even + 1, sb_ref[...], stats)

    _, acc_s = lax.cond(n_full % 2 == 1, odd_tail, lambda st: diag(j_even, sa_ref[...], st), stats)

    kwin = jnp.concatenate([kw0[0], kw1[0], kw2[0], kw3[0], kw4[0]], axis=0)
    vwin = jnp.concatenate([vw0[0], vw1[0], vw2[0], vw3[0], vw4[0]], axis=1)
    NW = WINDOW + QB
    kpos = q0 - WINDOW + lax.broadcasted_iota(jnp.int32, (NW, QB), 0)
    s_w = _dot(kwin, rhs_h) + lanes4(bw_ref[...] + jnp.where(kpos >= 0, 0.0, NEG))
    e_w = jnp.exp2(s_w - col_reduce(s_w, jnp.max))
    o_w = _dot(vwin, e_w.astype(BF16))

    def gate(c):
        return jnp.concatenate([g_ref[0, c, 0, r:r + 1, :] for r in range(R)], axis=1)

    o_t = (gate(0) * o_c + (gate(1) / acc_s[d:d + 1]) * acc_s[0:d]
           + (gate(2) / o_w[d:d + 1]) * o_w[0:d])
    o_ref[0] = jnp.concatenate(
        [jnp.concatenate([o_t[:, (2 * c) * QB:(2 * c + 1) * QB], o_t[:, (2 * c + 1) * QB:(2 * c + 2) * QB]],
                         axis=0).T for c in range(R // 2)], axis=1)


def nsa_attention_t(qh_t, ql_t, kch, kcl, vc_t, ks, vs_t, kw, vw_t, gates_t, batch, seq):
    W = qh_t.shape[0]
    G, R, d = NSA_GROUPS, NSA_REP, NSA_HEAD_DIM
    RW = R * d
    n_chunk = kch.shape[2]
    qpb = seq // Q_BLOCK
    wpb = WINDOW // Q_BLOCK
    n_cmp = n_chunk - CMP_LEN // CMP_STRIDE + 1
    n_sel = seq // SEL_LEN
    n = np.arange(n_chunk)[None, :]
    j = np.arange(n_sel)[:, None]
    overlap_t = ((n * CMP_STRIDE <= j * SEL_LEN + SEL_LEN - 1) & (n * CMP_STRIDE + CMP_LEN - 1 >= j * SEL_LEN)
                 & (n < n_cmp)).astype(np.float32)
    lc = jnp.concatenate([vc_t, jnp.broadcast_to(jnp.asarray(overlap_t, BF16), (batch, G, n_sel, n_chunk))],
                         axis=2)
    d = NSA_VROWS
    dist = np.arange(Q_BLOCK)[None, :] + WINDOW - np.arange(WINDOW + Q_BLOCK)[:, None]
    bias_w = np.where((dist >= 0) & (dist < WINDOW), 0.0, NEG).astype(np.float32)
    mb_rows = max(n_sel, SEL_TILE_BLOCKS)
    qblk = pl.BlockSpec((RW, Q_BLOCK), lambda b, g, i: (g, b * qpb + i))
    cblk = pl.BlockSpec((1, 1, n_chunk, LANES), lambda b, g, i: (b, g, 0, 0))
    wpos = lambda b, i, o: b * qpb + jnp.maximum(i + o - wpb, 0)
    kwblk = lambda o: pl.BlockSpec((1, Q_BLOCK, LANES), lambda b, g, i: (g, wpos(b, i, o), 0))
    vwblk = lambda o: pl.BlockSpec((1, d, Q_BLOCK), lambda b, g, i: (g, 0, wpos(b, i, o)))
    return pl.pallas_call(
        functools.partial(_nsa_attn_t_body, n_cmp=n_cmp),
        grid=(batch, G, qpb),
        in_specs=[qblk, qblk, cblk, cblk,
                  pl.BlockSpec((1, 1, d + n_sel, n_chunk), lambda b, g, i: (b, g, 0, 0)),
                  pl.BlockSpec((1, seq, LANES), lambda b, g, i: (g, b, 0)),
                  pl.BlockSpec((1, d, seq), lambda b, g, i: (g, 0, b))]
                 + [kwblk(o) for o in range(5)] + [vwblk(o) for o in range(5)]
                 + [pl.BlockSpec((1, 3, 1, R, Q_BLOCK), lambda b, g, i: (b, 0, g, 0, i)),
                    _full((WINDOW + Q_BLOCK, Q_BLOCK))],
        out_specs=pl.BlockSpec((1, Q_BLOCK, RW), lambda b, g, i: (b, i, g)),
        out_shape=jax.ShapeDtypeStruct((batch, seq, W), F32),
        scratch_shapes=[pltpu.VMEM((mb_rows, Q_BLOCK), F32)]
                       + [pltpu.VMEM((SEL_TILE_BLOCKS * SEL_LEN, R * Q_BLOCK), F32)] * 2,
        compiler_params=_cparams("parallel", "parallel", "arbitrary"),
        name="nsa_attention",
    )(qh_t, ql_t, kch, kcl, lc, ks, vs_t, *([kw] * 5), *([vw_t] * 5), gates_t, jnp.asarray(bias_w))


def nsa_mix_t(z_rest3, p, cos, sin):
    batch, seq, _ = z_rest3.shape
    G, R, d = NSA_GROUPS, NSA_REP, NSA_HEAD_DIM
    W = NSA_HEADS * d
    KW = G * d
    z2 = z_rest3.reshape(batch * seq, -1)
    qh_t, ql_t, ks, kw, vs_t, vw_t, gsig = nsa_prep(z2, p, seq, cos, sin)
    base = 2 * SEG + W
    raw = lambda i: z_rest3[..., base + i * KW:base + (i + 1) * KW]
    kch, kcl, vc = nsa_compress(raw(0), raw(1), p, batch, seq)
    n_chunk = vc.shape[2]
    ones = jnp.zeros((NSA_VROWS - d, n_chunk), BF16).at[0].set(1.0)
    vc_t = jnp.concatenate([vc[..., :d].transpose(0, 1, 3, 2),
                            jnp.broadcast_to(ones, (batch, G, NSA_VROWS - d, n_chunk))], axis=2)
    gates_t = (gsig[:, S_NSA:S_NSA + 3 * NSA_HEADS].reshape(batch, seq, G, R, 3).transpose(0, 4, 2, 3, 1))
    return nsa_attention_t(qh_t, ql_t, kch, kcl, vc_t, ks, vs_t, kw, vw_t, gates_t, batch, seq)


def _merge_body(h_ref, zg_ref, ya_ref, yb_ref, yc_ref, pa_ref, pb_ref, pc_ref, wo_ref, o_ref):
    D = D_MODEL

    def branch(i, y_ref, p_ref):
        gate = jax.nn.sigmoid(zg_ref[:, i * D:(i + 1) * D].astype(F32))
        return gate * _dot(y_ref[...].astype(BF16), p_ref[...])

    merged = branch(0, ya_ref, pa_ref) + branch(1, yb_ref, pb_ref) + branch(2, yc_ref, pc_ref)
    o_ref[...] = h_ref[...] + _dot(merged.astype(BF16), wo_ref[...])


def merge_out(h, z_gate, y_rw, y_gla, y_nsa, p, tm=256):
    T, D = h.shape
    W = y_rw.shape[1]
    row = lambda w: pl.BlockSpec((tm, w), lambda i: (i, 0))
    return pl.pallas_call(
        _merge_body,
        grid=(T // tm,),
        in_specs=[row(D), row(3 * D), row(W), row(W), row(W),
                  _full((W, D)), _full((W, D)), _full((W, D)), _full((D, D))],
        out_specs=row(D),
        out_shape=jax.ShapeDtypeStruct((T, D), F32),
        compiler_params=_cparams("parallel"),
        name="merge_out",
    )(h, z_gate, y_rw, y_gla, y_nsa, p['proj_a'].astype(BF16), p['proj_b'].astype(BF16),
      p['proj_c'].astype(BF16), p['w_out'].astype(BF16))


def _ffn_body(h_ref, g_ref, wg_ref, wu_ref, wd_ref, o_ref, xn_ref, acc_ref):
    f = pl.program_id(1)

    @pl.when(f == 0)
    def _():
        x = h_ref[...]
        ms = jnp.mean(x * x, axis=-1, keepdims=True)
        xn_ref[...] = (x * lax.rsqrt(ms + NORM_EPS) * g_ref[...]).astype(BF16)
        acc_ref[...] = jnp.zeros_like(acc_ref)

    xn = xn_ref[...]
    a = jax.nn.silu(_dot(xn, wg_ref[...])) * _dot(xn, wu_ref[...])
    acc_ref[...] += _dot(a.astype(BF16), wd_ref[...])

    @pl.when(f == pl.num_programs(1) - 1)
    def _():
        o_ref[...] = h_ref[...] + acc_ref[...]


def ffn_dense(h, p, tm=512, tf=1024):
    T, D = h.shape
    F0 = p['ffn_gate'].shape[1]
    F = -(-F0 // tf) * tf
    cols = lambda w: jnp.pad(w.astype(BF16), ((0, 0), (0, F - F0)))
    w_gate, w_up = cols(p['ffn_gate']), cols(p['ffn_up'])
    w_down = jnp.pad(p['ffn_down'].astype(BF16), ((0, F - F0), (0, 0)))
    return pl.pallas_call(
        _ffn_body,
        grid=(T // tm, F // tf),
        in_specs=[pl.BlockSpec((tm, D), lambda i, f: (i, 0)), _full((1, D)),
                  pl.BlockSpec((D, tf), lambda i, f: (0, f)), pl.BlockSpec((D, tf), lambda i, f: (0, f)),
                  pl.BlockSpec((tf, D), lambda i, f: (f, 0))],
        out_specs=pl.BlockSpec((tm, D), lambda i, f: (i, 0)),
        out_shape=jax.ShapeDtypeStruct((T, D), F32),
        scratch_shapes=[pltpu.VMEM((tm, D), BF16), pltpu.VMEM((tm, D), F32)],
        compiler_params=_cparams("parallel", "arbitrary"),
        name="ffn_dense",
    )(h, p['norm_ffn'].reshape(1, D), w_gate, w_up, w_down)


def _router_body(h_ref, g_ref, wr_ref, xn_ref, route_ref):
    x = h_ref[...]
    ms = jnp.mean(x * x, axis=-1, keepdims=True)
    xn = x * lax.rsqrt(ms + NORM_EPS) * g_ref[...]
    xn_ref[...] = xn.astype(BF16)
    logits = _dot(xn, wr_ref[...], HI)
    lane = lax.broadcasted_iota(jnp.int32, logits.shape, 1)
    valid = lane < N_EXPERTS
    probs = _masked_softmax(logits, valid)
    p1 = jnp.max(probs, axis=-1, keepdims=True)
    i1 = jnp.min(jnp.where(valid & (probs == p1), lane, LANES), axis=-1, keepdims=True)
    rest = jnp.where(valid & (lane != i1), probs, -1.0)
    p2 = jnp.max(rest, axis=-1, keepdims=True)
    i2 = jnp.min(jnp.where(rest == p2, lane, LANES), axis=-1, keepdims=True)
    tot = p1 + p2
    route_ref[...] = jnp.where(lane == 0, i1.astype(F32),
                               jnp.where(lane == 1, i2.astype(F32),
                                         jnp.where(lane == 2, p1 / tot, jnp.where(lane == 3, p2 / tot, 0.0))))


def moe_router(h, p, tm=512):
    T, D = h.shape
    wr = jnp.zeros((D, LANES), F32).at[:, :N_EXPERTS].set(p['router'])
    return pl.pallas_call(
        _router_body,
        grid=(T // tm,),
        in_specs=[pl.BlockSpec((tm, D), lambda i: (i, 0)), _full((1, D)), _full((D, LANES))],
        out_specs=[pl.BlockSpec((tm, D), lambda i: (i, 0)), pl.BlockSpec((tm, LANES), lambda i: (i, 0))],
        out_shape=[jax.ShapeDtypeStruct((T, D), BF16), jax.ShapeDtypeStruct((T, LANES), F32)],
        compiler_params=_cparams("parallel"),
        name="moe_router",
    )(h, p['norm_ffn'].reshape(1, D), wr)


def _moe_body(te_ref, tv_ref, x_ref, wg_ref, wu_ref, wd_ref, o_ref, acc_ref):
    t = pl.program_id(0)
    f = pl.program_id(1)

    @pl.when(f == 0)
    def _():
        acc_ref[...] = jnp.zeros_like(acc_ref)

    @pl.when(tv_ref[t] > 0)
    def _():
        x = x_ref[...]
        a = jax.nn.silu(_dot(x, wg_ref[0])) * _dot(x, wu_ref[0])
        acc_ref[...] += _dot(a.astype(BF16), wd_ref[0])

    @pl.when(f == pl.num_programs(1) - 1)
    def _():
        o_ref[...] = acc_ref[...].astype(o_ref.dtype)


def moe_grouped(x_sorted, tile_expert, tile_valid, p, tm, tf=1024):
    P, D = x_sorted.shape
    F = p['exp_gate'].shape[2]
    nf = F // tf

    def fidx(f, tv, t):
        return jnp.where(tv[t] > 0, f, nf - 1)

    grid_spec = pltpu.PrefetchScalarGridSpec(
        num_scalar_prefetch=2,
        grid=(P // tm, nf),
        in_specs=[pl.BlockSpec((tm, D), lambda t, f, te, tv: (t, 0)),
                  pl.BlockSpec((1, D, tf), lambda t, f, te, tv: (te[t], 0, fidx(f, tv, t))),
                  pl.BlockSpec((1, D, tf), lambda t, f, te, tv: (te[t], 0, fidx(f, tv, t))),
                  pl.BlockSpec((1, tf, D), lambda t, f, te, tv: (te[t], fidx(f, tv, t), 0))],
        out_specs=pl.BlockSpec((tm, D), lambda t, f, te, tv: (t, 0)),
        scratch_shapes=[pltpu.VMEM((tm, D), F32)],
    )
    return pl.pallas_call(
        _moe_body,
        grid_spec=grid_spec,
        out_shape=jax.ShapeDtypeStruct((P, D), BF16),
        compiler_params=_cparams("arbitrary", "arbitrary"),
        name="moe_grouped",
    )(tile_expert, tile_valid, x_sorted, p['exp_gate'].astype(BF16), p['exp_up'].astype(BF16),
      p['exp_down'].astype(BF16))


def _combine_body(h_ref, ya_ref, yb_ref, route_ref, o_ref):
    w = route_ref[...]
    o_ref[...] = h_ref[...] + w[:, 2:3] * ya_ref[...].astype(F32) + w[:, 3:4] * yb_ref[...].astype(F32)


def moe_combine(h, ya, yb, route, tm=512):
    T, D = h.shape
    row = lambda w: pl.BlockSpec((tm, w), lambda i: (i, 0))
    return pl.pallas_call(
        _combine_body,
        grid=(T // tm,),
        in_specs=[row(D), row(D), row(D), row(LANES)],
        out_specs=row(D),
        out_shape=jax.ShapeDtypeStruct((T, D), F32),
        compiler_params=_cparams("parallel"),
        name="moe_combine",
    )(h, ya, yb, route)


def moe_ffn(h, p, tm=512):
    T, D = h.shape
    xn, route = moe_router(h, p)
    experts = route[:, 0:TOP_K].astype(jnp.int32).reshape(-1)
    n_pairs = T * TOP_K
    n_tiles = n_pairs // tm + N_EXPERTS
    eid = jnp.arange(N_EXPERTS, dtype=jnp.int32)
    onehot = (experts[:, None] == eid[None, :]).astype(jnp.int32)
    running = jnp.cumsum(onehot, axis=0)
    counts = running[-1]
    tiles_e = (counts + tm - 1) // tm
    tile_end = jnp.cumsum(tiles_e)
    row_start = (tile_end - tiles_e) * tm
    pair_start = jnp.cumsum(counts) - counts
    dest = jnp.sum(onehot * (row_start[None, :] + running - 1), axis=1)
    order = jnp.argsort(experts, stable=True).astype(jnp.int32)
    tiles = jnp.arange(n_tiles, dtype=jnp.int32)
    tile_valid = (tiles < tile_end[-1]).astype(jnp.int32)
    tile_expert = jnp.minimum(jnp.sum((tiles[:, None] >= tile_end[None, :]).astype(jnp.int32), axis=1),
                              N_EXPERTS - 1)
    last_e = jnp.minimum(jnp.sum((tile_end[-1] - 1 >= tile_end).astype(jnp.int32)), N_EXPERTS - 1)
    tile_expert = jnp.where(tile_valid > 0, tile_expert, last_e)
    row_oh = (jnp.repeat(tile_expert, tm)[:, None] == eid[None, :]).astype(jnp.int32)
    k_in_e = jnp.arange(n_tiles * tm, dtype=jnp.int32) - jnp.sum(row_oh * row_start[None, :], axis=1)
    row_ok = (k_in_e < jnp.sum(row_oh * counts[None, :], axis=1)) & (jnp.repeat(tile_valid, tm) > 0)
    src = jnp.clip(jnp.sum(row_oh * pair_start[None, :], axis=1) + k_in_e, 0, n_pairs - 1)
    row_token = jnp.where(row_ok, order[src] // TOP_K, 0)
    y_sorted = moe_grouped(xn[row_token], tile_expert, tile_valid, p, tm)
    dest = dest.reshape(T, TOP_K)
    return moe_combine(h, y_sorted[dest[:, 0]], y_sorted[dest[:, 1]], route)


def _reorder_w_in(w_in, has_vres):
    D = D_MODEL
    n_rw = 3 * RW_WIDTH + 160 + (32 if has_vres else 0)
    rw0 = 3 * D
    gla0 = rw0 + n_rw
    nsa0 = gla0 + 1552
    col = lambda a, b: w_in[:, a:b]
    zeros = lambda n: jnp.zeros((D, n), w_in.dtype)
    small = [col(rw0 + 1536, rw0 + n_rw)] + ([] if has_vres else [zeros(32)])
    small += [col(gla0 + 1024, gla0 + 1040), col(nsa0 + 1280, nsa0 + 1304), zeros(SMALL - S_NSA - 24)]
    rest = [col(rw0, rw0 + 1536), col(gla0, gla0 + 1024), col(gla0 + 1040, gla0 + 1552),
            col(nsa0, nsa0 + 1280)] + small
    return w_in[:, :rw0].astype(BF16), jnp.concatenate(rest, axis=1).astype(BF16)


def _layer(h, p, v_first, batch, seq, cos, sin, is_moe):
    T = batch * seq
    w_gate, w_rest = _reorder_w_in(p['w_in'], v_first is not None)
    z_gate = norm_matmul(h, p['norm_mix'], w_gate, BF16, 1024, 1536)
    z_rest = norm_matmul(h, p['norm_mix'], w_rest, F32, 1024, 1536)
    z_rest3 = z_rest.reshape(batch, seq, -1)
    prep = rwkv_prep(z_rest, v_first, p, seq)
    if v_first is None:
        v_first = prep[3]
    y_rw = rwkv_chunk(prep, p, batch, seq).reshape(T, -1)
    y_gla = gla_mix(z_rest3, p).reshape(T, -1)
    y_nsa = nsa_mix_t(z_rest3, p, cos, sin).reshape(T, -1)
    h = merge_out(h, z_gate, y_rw, y_gla, y_nsa, p)
    h = moe_ffn(h, p) if is_moe else ffn_dense(h, p)
    return h, v_first


_L0 = ('norm_mix', 'w_in', 'rw_mu', 'rw_w0', 'rw_w2', 'rw_a0', 'rw_a2', 'rw_g2', 'rw_kk', 'rw_ka', 'rw_rk',
       'rw_ln_w', 'rw_ln_b', 'gla_wg', 'gla_bg', 'gla_norm', 'nsa_qnorm', 'nsa_kcnorm', 'nsa_ksnorm',
       'nsa_kwnorm', 'nsa_pe_k', 'nsa_c1_k', 'nsa_c2_k', 'nsa_pe_v', 'nsa_c1_v', 'nsa_c2_v', 'proj_a', 'proj_b',
       'proj_c', 'w_out', 'norm_ffn', 'ffn_gate', 'ffn_up', 'ffn_down')
_L1 = ('norm_mix', 'w_in', 'rw_mu', 'rw_w0', 'rw_w2', 'rw_a0', 'rw_a2', 'rw_g2', 'rw_v0', 'rw_v2', 'rw_kk',
       'rw_ka', 'rw_rk', 'rw_ln_w', 'rw_ln_b', 'gla_wg', 'gla_bg', 'gla_norm', 'nsa_qnorm', 'nsa_kcnorm',
       'nsa_ksnorm', 'nsa_kwnorm', 'nsa_pe_k', 'nsa_c1_k', 'nsa_c2_k', 'nsa_pe_v', 'nsa_c1_v', 'nsa_c2_v',
       'proj_a', 'proj_b', 'proj_c', 'w_out', 'norm_ffn', 'router', 'exp_gate', 'exp_up', 'exp_down')


def kernel(x, *weights):
    p0 = dict(zip(_L0, weights[:len(_L0)]))
    p1 = dict(zip(_L1, weights[len(_L0):]))
    batch, seq, D = x.shape
    cos, sin = _rope_tables(jnp.arange(seq), NSA_HEADS * NSA_HEAD_DIM)
    h = x.reshape(batch * seq, D)
    h, v_first = _layer(h, p0, None, batch, seq, cos, sin, False)
    h, _ = _layer(h, p1, v_first, batch, seq, cos, sin, True)
    return h.reshape(batch, seq, D)
```

```python
import functools

import numpy as np
import jax
import jax.numpy as jnp
from jax import lax
from jax.experimental import pallas as pl
from jax.experimental.pallas import tpu as pltpu

F32 = jnp.float32
BF16 = jnp.bfloat16
HI = lax.Precision.HIGHEST

D_MODEL = 2048
RW_HEADS = 8
RW_HEAD_DIM = 64
RW_WIDTH = 512
RW_GN_EPS = 64e-5
GLA_HEADS = 4
GLA_DK = 64
GLA_DV = 128
GLA_TAU = 16.0
CHUNK = 64
NSA_HEADS = 8
NSA_GROUPS = 2
NSA_REP = NSA_HEADS // NSA_GROUPS
NSA_HEAD_DIM = 64
KV_REP = 2
SEL_TILE_BLOCKS = 16
NSA_VROWS = NSA_HEAD_DIM + 16
CMP_LEN = 32
CMP_STRIDE = 16
SEL_LEN = 64
SEL_TOPK = 16
WINDOW = 512
Q_BLOCK = 128
ROPE_THETA = 10000.0
N_EXPERTS = 8
TOP_K = 2
NORM_EPS = 1e-6

LANES = 128
VMEM_LIMIT = 56 * 1024 * 1024

SEG = 1536
SMALL = 256
SMALL_OFF = 3 * SEG - SMALL
S_ZW, S_ZA, S_ZG, S_ZV, S_GLA, S_NSA = 0, 32, 64, 160, 192, 208
NEG = -1e30
LOG2E = 1.4426950408889634


def _dot(a, b, precision=None):
    return jnp.dot(a, b, preferred_element_type=F32, precision=precision)


def _dot_nt(a, b, precision=None):
    return lax.dot_general(a, b, (((1,), (1,)), ((), ())), preferred_element_type=F32, precision=precision)


def _dot_tn(a, b, precision=None):
    return lax.dot_general(a, b, (((0,), (0,)), ((), ())), preferred_element_type=F32, precision=precision)


def _split_bf16(x):
    hi = x.astype(BF16)
    return hi, (x - hi.astype(F32)).astype(BF16)


def _dot_rhs_exact(a, b):
    hi, lo = _split_bf16(a)
    return _dot(hi, b) + _dot(lo, b)


def _dot_lhs_exact(a, b):
    hi, lo = _split_bf16(b)
    return _dot(a, hi) + _dot(a, lo)


def _dot_split(a, b_hi, b_lo):
    hi, lo = _split_bf16(a)
    return _dot(hi, b_hi) + _dot(hi, b_lo) + _dot(lo, b_hi)


def _cparams(*sem, flags=None):
    return pltpu.CompilerParams(dimension_semantics=sem, vmem_limit_bytes=VMEM_LIMIT, flags=flags)


def _full(shape):
    n = len(shape)
    return pl.BlockSpec(shape, lambda *_: (0,) * n)


CAST_BLOCK_BYTES = 8 * 1024 * 1024


def _cast_body(x_ref, o_ref):
    o_ref[...] = x_ref[...].astype(o_ref.dtype)


def cast_bf16(w):
    shape = w.shape
    rows_all, cols = shape[-2], shape[-1]
    w3 = w.reshape(-1, rows_all, cols)
    rows = next(r for r in (1024, 512, 256, 128, 64, 32, 16)
                if rows_all % r == 0 and r * cols * w.dtype.itemsize <= CAST_BLOCK_BYTES)
    blk = pl.BlockSpec((1, rows, cols), lambda e, i: (e, i, 0))
    out = pl.pallas_call(
        _cast_body,
        grid=(w3.shape[0], rows_all // rows),
        in_specs=[blk],
        out_specs=blk,
        out_shape=jax.ShapeDtypeStruct(w3.shape, BF16),
        compiler_params=_cparams("parallel", "parallel"),
        name="cast_bf16",
    )(w3)
    return out.reshape(shape)


def _norm_matmul_body(x_ref, g_ref, w_ref, o_ref, xn_ref):
    @pl.when(pl.program_id(1) == 0)
    def _():
        x = x_ref[...]
        ms = jnp.mean(x * x, axis=-1, keepdims=True)
        xn_ref[...] = (x * lax.rsqrt(ms + NORM_EPS) * g_ref[...]).astype(BF16)

    o_ref[...] = _dot(xn_ref[...], w_ref[...]).astype(o_ref.dtype)


def norm_matmul(x, g, w, out_dtype, tm, tn):
    M, K = x.shape
    N = w.shape[1]
    return pl.pallas_call(
        _norm_matmul_body,
        grid=(M // tm, N // tn),
        in_specs=[pl.BlockSpec((tm, K), lambda i, j: (i, 0)),
                  pl.BlockSpec((1, K), lambda i, j: (0, 0)),
                  pl.BlockSpec((K, tn), lambda i, j: (0, j))],
        out_specs=pl.BlockSpec((tm, tn), lambda i, j: (i, j)),
        out_shape=jax.ShapeDtypeStruct((M, N), out_dtype),
        scratch_shapes=[pltpu.VMEM((tm, K), BF16)],
        compiler_params=_cparams("parallel", "arbitrary"),
        name="norm_matmul",
    )(x, g.reshape(1, K), w)


def _shift_rows(x, prev_row):
    rolled = pltpu.roll(x, 1, axis=0)
    first = lax.broadcasted_iota(jnp.int32, x.shape, 0) == 0
    return jnp.where(first, prev_row, rolled)


def _rwkv_prep_body(*refs, tiles_per_seq, has_vres):
    if has_vres:
        (zm_ref, zs_ref, vf_ref, mum_ref, mus_ref, vec_ref, w2_ref, a2_ref, g2_ref, v2_ref, bd_ref,
         r_ref, lw_ref, k_ref, v_ref, kk_ref, b_ref, g_ref, bonus_ref, pm_ref, ps_ref) = refs
    else:
        (zm_ref, zs_ref, mum_ref, mus_ref, vec_ref, w2_ref, a2_ref, g2_ref, bd_ref,
         r_ref, lw_ref, k_ref, v_ref, kk_ref, b_ref, g_ref, bonus_ref, pm_ref, ps_ref) = refs
    i = pl.program_id(0)

    @pl.when(i % tiles_per_seq == 0)
    def _():
        pm_ref[...] = jnp.zeros_like(pm_ref)
        ps_ref[...] = jnp.zeros_like(ps_ref)

    zm = zm_ref[...]
    zs = zs_ref[...]
    tm = zm.shape[0]
    zm_prev = _shift_rows(zm, pm_ref[0:1, :])
    zs_prev = _shift_rows(zs, ps_ref[0:1, :])
    pm_ref[0:1, :] = zm[tm - 1:tm, :]
    ps_ref[0:1, :] = zs[tm - 1:tm, :]
    zm = zm + (zm_prev - zm) * mum_ref[...]
    zs = zs + (zs_prev - zs) * mus_ref[...]

    r = zm[:, 0:RW_WIDTH]
    k = zm[:, RW_WIDTH:2 * RW_WIDTH]
    v = zm[:, 2 * RW_WIDTH:3 * RW_WIDTH]
    w0 = vec_ref[0:1, :]
    a0 = vec_ref[1:2, :]
    v0 = vec_ref[2:3, :]
    kkw = vec_ref[3:4, :]
    ka = vec_ref[4:5, :]
    rk = vec_ref[5:6, :]

    w_log = -jax.nn.softplus(-(w0 + _dot_split(jnp.tanh(zs), w2_ref[0], w2_ref[1]))) - 0.5
    lw = -jnp.exp(w_log)
    a = jax.nn.sigmoid(a0 + _dot_split(zs, a2_ref[0], a2_ref[1]))
    g = _dot_split(jax.nn.sigmoid(zs), g2_ref[0], g2_ref[1])
    if has_vres:
        v = v + (vf_ref[...] - v) * jax.nn.sigmoid(v0 + _dot_split(zs, v2_ref[0], v2_ref[1]))

    bd = bd_ref[...]
    kk = k * kkw
    nrm = jnp.sqrt(_dot_rhs_exact(kk * kk, bd))
    kk = kk / jnp.maximum(nrm, 1e-12)
    k2 = k * (1.0 + (a - 1.0) * ka)
    bonus = _dot_rhs_exact(r * k2 * rk, bd) * v

    r_ref[...] = r
    lw_ref[...] = lw
    k_ref[...] = k2
    v_ref[...] = v
    kk_ref[...] = kk
    b_ref[...] = kk * a
    g_ref[...] = g
    bonus_ref[...] = bonus


def _pad_rows(w, off, total=SMALL):
    hi, lo = _split_bf16(jnp.zeros((total, w.shape[1]), F32).at[off:off + w.shape[0]].set(w))
    return jnp.stack([hi, lo])


def rwkv_prep(z_rest, v_first, p, seq, tm=256):
    T = z_rest.shape[0]
    has_vres = v_first is not None
    mu = p['rw_mu']
    n_small = mu.shape[0] - 3 * RW_WIDTH
    mum = mu[:3 * RW_WIDTH].reshape(1, -1)
    mus = jnp.zeros((SMALL,), F32).at[:n_small].set(mu[3 * RW_WIDTH:]).reshape(1, -1)
    zero = jnp.zeros((RW_WIDTH,), F32)
    vec = jnp.stack([p['rw_w0'], p['rw_a0'], p['rw_v0'] if has_vres else zero, p['rw_kk'], p['rw_ka'],
                     p['rw_rk'].reshape(-1), zero, zero])
    w2 = _pad_rows(p['rw_w2'], S_ZW)
    a2 = _pad_rows(p['rw_a2'], S_ZA)
    g2 = _pad_rows(p['rw_g2'], S_ZG)
    hid = np.arange(RW_WIDTH) // RW_HEAD_DIM
    bd = jnp.asarray((hid[:, None] == hid[None, :]).astype(np.float32), BF16)
    row = lambda w: pl.BlockSpec((tm, w), lambda i: (i, 0))
    lora = _full((2, SMALL, RW_WIDTH))
    in_specs = [row(SEG), pl.BlockSpec((tm, SMALL), lambda i: (i, SMALL_OFF // SMALL))]
    args = [z_rest, z_rest]
    if has_vres:
        in_specs.append(row(RW_WIDTH))
        args.append(v_first)
    in_specs += [_full((1, SEG)), _full((1, SMALL)), _full((8, RW_WIDTH)), lora, lora, lora]
    args += [mum, mus, vec, w2, a2, g2]
    if has_vres:
        in_specs.append(lora)
        args.append(_pad_rows(p['rw_v2'], S_ZV))
    in_specs.append(_full((RW_WIDTH, RW_WIDTH)))
    args.append(bd)
    out = jax.ShapeDtypeStruct((T, RW_WIDTH), F32)
    return pl.pallas_call(
        functools.partial(_rwkv_prep_body, tiles_per_seq=seq // tm, has_vres=has_vres),
        grid=(T // tm,),
        in_specs=in_specs,
        out_specs=[row(RW_WIDTH)] * 8,
        out_shape=[out] * 8,
        scratch_shapes=[pltpu.VMEM((8, SEG), F32), pltpu.VMEM((8, SMALL), F32)],
        compiler_params=_cparams("arbitrary"),
        name="rwkv_prep",
    )(*args)


def _rwkv_chunk_body(r_ref, lw_ref, k_ref, v_ref, kk_ref, b_ref, g_ref, bonus_ref, lnw_ref, lnb_ref,
                     o_ref, state_ref, *, n_chunks):
    C = CHUNK
    N = RW_HEAD_DIM

    @pl.when(pl.program_id(1) == 0)
    def _():
        state_ref[...] = jnp.zeros_like(state_ref)

    ti = lax.broadcasted_iota(jnp.int32, (C, C), 0)
    si = lax.broadcasted_iota(jnp.int32, (C, C), 1)
    incl = si <= ti
    strict = si < ti
    tril1 = incl.astype(BF16)
    eye = (si == ti).astype(F32)
    masks = []
    m = 1
    while m < C:
        masks.append(((ti // (2 * m)) == (si // (2 * m))) & ((ti // m) % 2 == 1) & ((si // m) % 2 == 0))
        m *= 2
    lnw = lnw_ref[...]
    lnb = lnb_ref[...]

    NCH = range(n_chunks)
    H = range(RW_HEADS)
    U = [(c, h) for c in NCH for h in H]
    sl = [slice(h * N, (h + 1) * N) for h in H]
    ew = []
    for c in NCH:
        rows = pl.ds(c * C, C)
        lw = lw_ref[0, rows, :]
        k = k_ref[0, rows, :]
        b = b_ref[0, rows, :]
        cum = _dot_lhs_exact(tril1, lw)
        cum_last = cum[C - 1:C, :]
        e_neg = jnp.exp(-cum)
        e_rel = jnp.exp(cum_last - cum)
        ew.append(dict(
            rows=rows,
            alpha=(kk_ref[0, rows, :] * jnp.exp(cum - lw)).astype(BF16),
            beta=(b * e_neg).astype(BF16), kappa=(k * e_neg).astype(BF16),
            rho=(r_ref[0, rows, :] * jnp.exp(cum)).astype(BF16),
            beta2=(b * e_rel).astype(BF16), kappa2=(k * e_rel).astype(BF16),
            pc=jnp.exp(cum_last), v=v_ref[0, rows, :].astype(BF16)))
    A = [ew[c]['alpha'][:, sl[h]] for c, h in U]
    Rh = [ew[c]['rho'][:, sl[h]] for c, h in U]
    V = [ew[c]['v'][:, sl[h]] for c, h in U]
    p4 = [_dot_nt(jnp.concatenate([A[u], Rh[u]], axis=0),
                  jnp.concatenate([ew[c]['beta'][:, sl[h]], ew[c]['kappa'][:, sl[h]]], axis=0))
          for u, (c, h) in enumerate(U)]
    L = [jnp.where(strict, p[:C, :C], 0.0) for p in p4]
    T = [eye for _ in U]
    for mk in masks:
        n = [_dot(T[u].astype(BF16), jnp.where(mk, L[u], 0.0).astype(BF16)) for u in range(len(U))]
        T = [T[u] - _dot(n[u].astype(BF16), T[u].astype(BF16)) for u in range(len(U))]
    T = [t.astype(BF16) for t in T]
    kv = [_dot(jnp.concatenate([jnp.where(strict, p[:C, C:], 0.0), jnp.where(incl, p[C:, C:], 0.0)],
                               axis=0).astype(BF16), V[u]) for u, p in enumerate(p4)]
    lhs = [jnp.concatenate([_dot(T[u], A[u]).astype(BF16), Rh[u]], axis=0) for u in range(len(U))]
    W = [_dot(T[u], kv[u][:C].astype(BF16)) for u in range(len(U))]
    D = [_dot_tn(V[u], ew[c]['kappa2'][:, sl[h]]) for u, (c, h) in enumerate(U)]
    G = [jnp.where(incl, p[C:, :C], 0.0).astype(BF16) for p in p4]

    S = [state_ref[h] for h in H]
    for c in NCH:
        u0 = c * RW_HEADS
        rs = [_dot_nt(lhs[u0 + h], S[h].astype(BF16)) for h in H]
        Ub = [(-(rs[h][:C] + W[u0 + h])).astype(BF16) for h in H]
        Y = [rs[h][C:] + _dot(G[u0 + h], Ub[h]) + kv[u0 + h][C:] for h in H]
        S = [S[h] * ew[c]['pc'][:, sl[h]] + _dot_tn(Ub[h], ew[c]['beta2'][:, sl[h]]) + D[u0 + h] for h in H]
        yn = []
        for h in H:
            mu = jnp.mean(Y[h], axis=-1, keepdims=True)
            yc = Y[h] - mu
            var = jnp.mean(yc * yc, axis=-1, keepdims=True)
            yn.append(yc * lax.rsqrt(var + RW_GN_EPS))
        rows = ew[c]['rows']
        o_ref[0, rows, :] = ((jnp.concatenate(yn, axis=1) * lnw + lnb + bonus_ref[0, rows, :])
                             * g_ref[0, rows, :])
    state_ref[...] = jnp.stack(S)


def rwkv_chunk(prep, p, batch, seq, rows=256):
    r, lw, k2, v, kk, b, g, bonus = [t.reshape(batch, seq, RW_WIDTH) for t in prep]
    blk = pl.BlockSpec((1, rows, RW_WIDTH), lambda bi, i: (bi, i, 0))
    return pl.pallas_call(
        functools.partial(_rwkv_chunk_body, n_chunks=rows // CHUNK),
        grid=(batch, seq // rows),
        in_specs=[blk] * 8 + [_full((1, RW_WIDTH))] * 2,
        out_specs=blk,
        out_shape=jax.ShapeDtypeStruct((batch, seq, RW_WIDTH), F32),
        scratch_shapes=[pltpu.VMEM((RW_HEADS, RW_HEAD_DIM, RW_HEAD_DIM), F32)],
        compiler_params=_cparams("parallel", "arbitrary"),
        name="rwkv_chunk",
    )(r, lw, k2, v, kk, b, g, bonus, p['rw_ln_w'].reshape(1, -1), p['rw_ln_b'].reshape(1, -1))


def _gla_body(z_ref, zs_ref, wg_ref, bg_ref, gn_ref, o_ref, state_ref, *, n_chunks):
    C = CHUNK
    HK = GLA_HEADS * GLA_DK
    HV = GLA_HEADS * GLA_DV

    @pl.when(pl.program_id(1) == 0)
    def _():
        state_ref[...] = jnp.zeros_like(state_ref)

    RB = n_chunks * C
    ti = lax.broadcasted_iota(jnp.int32, (RB, RB), 0)
    si = lax.broadcasted_iota(jnp.int32, (RB, RB), 1)
    tril_blk = ((si <= ti) & (si // C == ti // C)).astype(BF16)
    causal = (lax.broadcasted_iota(jnp.int32, (C, HK), 1) % GLA_DK) <= lax.broadcasted_iota(jnp.int32, (C, HK), 0)
    blk_k = (lax.broadcasted_iota(jnp.int32, (HK, HK), 0) // GLA_DK) == (lax.broadcasted_iota(jnp.int32, (HK, HK), 1) // GLA_DK)
    blk_v = (lax.broadcasted_iota(jnp.int32, (HK, HV), 0) // GLA_DK) == (lax.broadcasted_iota(jnp.int32, (HK, HV), 1) // GLA_DV)
    blk_vt = (lax.broadcasted_iota(jnp.int32, (HV, HK), 0) // GLA_DV) == (lax.broadcasted_iota(jnp.int32, (HV, HK), 1) // GLA_DK)

    z = z_ref[0]
    log_a = jax.nn.log_sigmoid(_dot_split(zs_ref[0], wg_ref[0], wg_ref[1]) + bg_ref[...]) / GLA_TAU
    bcum_all = _dot_lhs_exact(tril_blk, log_a)
    pre = []
    for c in range(n_chunks):
        rows = slice(c * C, (c + 1) * C)
        q = z[rows, 0:HK] * (GLA_DK ** -0.5)
        k = z[rows, HK:2 * HK]
        vb = z[rows, 2 * HK:2 * HK + HV].astype(BF16)
        bcum = bcum_all[rows]
        b_last = bcum[C - 1:C, :]
        q_dec = (q * jnp.exp(bcum)).astype(BF16)
        k_inv = (k * jnp.exp(-bcum)).astype(BF16)
        k_dec = (k * jnp.exp(b_last - bcum)).astype(BF16)
        kbd = jnp.where(blk_k, jnp.concatenate([k_inv] * GLA_HEADS, axis=0), jnp.zeros((), BF16))
        att = jnp.where(causal, _dot_nt(q_dec, kbd), 0.0)
        vbd = jnp.where(blk_v, jnp.concatenate([vb] * GLA_HEADS, axis=0), jnp.zeros((), BF16))
        pre.append((q_dec, _dot(att.astype(BF16), vbd), jnp.exp(b_last),
                    jnp.where(blk_vt, _dot_tn(vb, k_dec), 0.0)))
    st = state_ref[...]
    outs = []
    for q_dec, o_intra, decay, inc in pre:
        outs.append(o_intra + _dot_nt(q_dec, st.astype(BF16)))
        st = st * decay + inc
    state_ref[...] = st
    o = jnp.concatenate(outs, axis=0)
    parts = []
    for h in range(GLA_HEADS):
        oh = o[:, h * GLA_DV:(h + 1) * GLA_DV]
        ms = jnp.mean(oh * oh, axis=-1, keepdims=True)
        parts.append(oh * lax.rsqrt(ms + NORM_EPS))
    on = jnp.concatenate(parts, axis=-1) * gn_ref[...]
    o_ref[0] = on * jax.nn.silu(z[:, 2 * HK + HV:2 * HK + 2 * HV])


def gla_mix(z_rest3, p, rows=256):
    batch, seq, _ = z_rest3.shape
    HV = GLA_HEADS * GLA_DV
    wg = _pad_rows(p['gla_wg'], S_GLA)
    return pl.pallas_call(
        functools.partial(_gla_body, n_chunks=rows // CHUNK),
        grid=(batch, seq // rows),
        in_specs=[pl.BlockSpec((1, rows, SEG), lambda b, i: (b, i, 1)),
                  pl.BlockSpec((1, rows, SMALL), lambda b, i: (b, i, SMALL_OFF // SMALL)),
                  _full((2, SMALL, GLA_HEADS * GLA_DK)), _full((1, GLA_HEADS * GLA_DK)), _full((1, HV))],
        out_specs=pl.BlockSpec((1, rows, HV), lambda b, i: (b, i, 0)),
        out_shape=jax.ShapeDtypeStruct((batch, seq, HV), F32),
        scratch_shapes=[pltpu.VMEM((HV, GLA_HEADS * GLA_DK), F32)],
        compiler_params=_cparams("parallel", "arbitrary"),
        name="gla_mix",
    )(z_rest3, z_rest3, wg, p['gla_bg'].reshape(1, -1), jnp.tile(p['gla_norm'], GLA_HEADS).reshape(1, -1))


def _rope(x, cos, sin_signed):
    w = x.shape[-1]
    lane = lax.broadcasted_iota(jnp.int32, x.shape, 1) % NSA_HEAD_DIM
    partner = jnp.where(lane < NSA_HEAD_DIM // 2, pltpu.roll(x, w - NSA_HEAD_DIM // 2, axis=1),
                        pltpu.roll(x, NSA_HEAD_DIM // 2, axis=1))
    return x * cos + partner * sin_signed


def _nsa_prep_body(z_ref, cos_ref, sin_ref, qn_ref, kn_ref, bd_ref,
                   qh_ref, ql_ref, ks_ref, kw_ref, vs_ref, vw_ref, g_ref, *, tiles_per_seq):
    z = z_ref[...]
    cos = cos_ref[...]
    sin = sin_ref[...]
    bd = bd_ref[...]
    d = NSA_HEAD_DIM
    W = NSA_HEADS * d
    KW = NSA_GROUPS * d
    tm = z.shape[0]

    def head_norm(x, g, n):
        ms = _dot_rhs_exact(x * x, bd[:n, :n]) * (1.0 / d)
        return x * lax.rsqrt(ms + NORM_EPS) * g

    q = head_norm(z[:, 0:W], qn_ref[...], W)
    qh_ref[...], ql_ref[...] = _split_bf16((_rope(q, cos, sin) * (d ** -0.5 * LOG2E)).T)

    lane = lax.broadcasted_iota(jnp.int32, (tm, LANES), 1)
    pos = (pl.program_id(0) % tiles_per_seq) * tm + lax.broadcasted_iota(jnp.int32, (tm, LANES), 0)
    onehot = jnp.where(lane - d == (pos // SEL_LEN) % SEL_TILE_BLOCKS, 1.0, 0.0)

    def keys(k, extra, ref):
        for g in range(NSA_GROUPS):
            kg = k if g == 0 else pltpu.roll(k, LANES - g * d, axis=1)
            ref[g] = jnp.where(lane < d, kg, extra).astype(BF16)

    def values_t(v, ref):
        vt = v.T
        tail = jnp.where(lax.broadcasted_iota(jnp.int32, (NSA_VROWS - d, tm), 0) == 0, 1.0, 0.0)
        for g in range(NSA_GROUPS):
            ref[g] = jnp.concatenate([vt[g * d:(g + 1) * d], tail], axis=0).astype(BF16)

    ks = head_norm(z[:, W + 2 * KW:W + 3 * KW], kn_ref[0:1, :], KW)
    keys(_rope(ks, cos[:, :KW], sin[:, :KW]), onehot, ks_ref)
    kw = head_norm(z[:, W + 4 * KW:W + 5 * KW], kn_ref[1:2, :], KW)
    keys(_rope(kw, cos[:, :KW], sin[:, :KW]), jnp.zeros((tm, LANES), F32), kw_ref)
    values_t(z[:, W + 3 * KW:W + 4 * KW], vs_ref)
    values_t(z[:, W + 5 * KW:W + 6 * KW], vw_ref)
    g_ref[...] = jax.nn.sigmoid(z[:, SEG - SMALL:SEG])


def _rope_tables(pos, width):
    inv = 1.0 / (ROPE_THETA ** (jnp.arange(0, NSA_HEAD_DIM, 2, dtype=F32) / NSA_HEAD_DIM))
    ang = pos.astype(F32)[:, None] * inv[None, :]
    c, s = jnp.cos(ang), jnp.sin(ang)
    reps = width // NSA_HEAD_DIM
    return jnp.tile(jnp.concatenate([c, c], -1), (1, reps)), jnp.tile(jnp.concatenate([-s, s], -1), (1, reps))


def nsa_prep(z_rest, p, seq, cos, sin, tm=256):
    T = z_rest.shape[0]
    W = NSA_HEADS * NSA_HEAD_DIM
    KW = NSA_GROUPS * NSA_HEAD_DIM
    hid = np.arange(W) // NSA_HEAD_DIM
    bd = jnp.asarray((hid[:, None] == hid[None, :]).astype(np.float32), BF16)
    qn = jnp.tile(p['nsa_qnorm'], NSA_HEADS).reshape(1, W)
    kn = jnp.stack([jnp.tile(p['nsa_ksnorm'], NSA_GROUPS), jnp.tile(p['nsa_kwnorm'], NSA_GROUPS)])
    tps = seq // tm
    tab = pl.BlockSpec((tm, W), lambda i: (i % tps, 0))
    G = NSA_GROUPS
    q_spec = pl.BlockSpec((W, tm), lambda i: (0, i))
    k_spec = pl.BlockSpec((G, tm, LANES), lambda i: (0, i, 0))
    v_spec = pl.BlockSpec((G, NSA_VROWS, tm), lambda i: (0, 0, i))
    q_shape = jax.ShapeDtypeStruct((W, T), BF16)
    k_shape = jax.ShapeDtypeStruct((G, T, LANES), BF16)
    v_shape = jax.ShapeDtypeStruct((G, NSA_VROWS, T), BF16)
    return pl.pallas_call(
        functools.partial(_nsa_prep_body, tiles_per_seq=tps),
        grid=(T // tm,),
        in_specs=[pl.BlockSpec((tm, SEG), lambda i: (i, 2)), tab, tab,
                  _full((1, W)), _full((2, KW)), _full((W, W))],
        out_specs=[q_spec, q_spec, k_spec, k_spec, v_spec, v_spec, pl.BlockSpec((tm, SMALL), lambda i: (i, 0))],
        out_shape=[q_shape, q_shape, k_shape, k_shape, v_shape, v_shape, jax.ShapeDtypeStruct((T, SMALL), F32)],
        compiler_params=_cparams("parallel"),
        name="nsa_prep",
    )(z_rest, cos, sin, qn, kn, bd)


def _nsa_compress_body(xk_ref, xv_ref, pek_ref, pev_ref, w1k_ref, w1v_ref, w2k_ref, w2v_ref, kn_ref,
                       cos_ref, sin_ref, kch_ref, kcl_ref, vc_ref):
    def compress(x, pe_ref, w1_ref, w2_ref):
        a = _dot((x + pe_ref[0:1, :]).astype(BF16), w1_ref[0])
        b = _dot((x + pe_ref[1:2, :]).astype(BF16), w1_ref[1])
        n = a.shape[0]
        h = jax.nn.gelu(a + pltpu.roll(b, n - 1, axis=0))
        return _dot(h.astype(BF16), w2_ref[...])

    kc = compress(xk_ref[0, 0], pek_ref, w1k_ref, w2k_ref)
    ms = jnp.mean(kc * kc, axis=-1, keepdims=True)
    kc = kc * lax.rsqrt(ms + NORM_EPS) * kn_ref[...]
    kc = kc * cos_ref[...] + pltpu.roll(kc, NSA_HEAD_DIM // 2, axis=1) * sin_ref[...]
    kch_ref[0, 0], kcl_ref[0, 0] = _split_bf16(kc)
    vc_ref[0, 0] = compress(xv_ref[0, 0], pev_ref, w1v_ref, w2v_ref).astype(BF16)


def nsa_compress(kc_raw, vc_raw, p, batch, seq):
    G, d = NSA_GROUPS, NSA_HEAD_DIM
    n_chunk = seq // CMP_STRIDE
    RW = KV_REP * d

    def chunks(t):
        t = t.reshape(batch, n_chunk, CMP_STRIDE, G, d).transpose(0, 3, 1, 2, 4)
        return t.reshape(batch, G, n_chunk, CMP_STRIDE * d)

    def w1(t):
        return t.reshape(2, CMP_STRIDE * d, d).astype(BF16)

    def pe(t):
        return t.reshape(2, CMP_STRIDE * d)

    c_end = jnp.arange(n_chunk) * CMP_STRIDE + (CMP_LEN - 1)
    cos_c, sin_c = _rope_tables(c_end, RW)
    blk = pl.BlockSpec((1, 1, n_chunk, CMP_STRIDE * d), lambda b, g: (b, g, 0, 0))
    oblk = pl.BlockSpec((1, 1, n_chunk, RW), lambda b, g: (b, g, 0, 0))
    return pl.pallas_call(
        _nsa_compress_body,
        grid=(batch, G),
        in_specs=[blk, blk, _full((2, CMP_STRIDE * d)), _full((2, CMP_STRIDE * d)),
                  _full((2, CMP_STRIDE * d, d)), _full((2, CMP_STRIDE * d, d)),
                  _full((d, RW)), _full((d, RW)), _full((1, RW)), _full((n_chunk, RW)), _full((n_chunk, RW))],
        out_specs=[oblk, oblk, oblk],
        out_shape=[jax.ShapeDtypeStruct((batch, G, n_chunk, RW), BF16)] * 3,
        compiler_params=_cparams("parallel", "parallel"),
        name="nsa_compress",
    )(chunks(kc_raw), chunks(vc_raw), pe(p['nsa_pe_k']), pe(p['nsa_pe_v']), w1(p['nsa_c1_k']), w1(p['nsa_c1_v']),
      jnp.tile(p['nsa_c2_k'], (1, KV_REP)).astype(BF16), jnp.tile(p['nsa_c2_v'], (1, KV_REP)).astype(BF16),
      jnp.tile(p['nsa_kcnorm'], KV_REP).reshape(1, RW), cos_c, sin_c)


def _masked_softmax(s, valid):
    m = jnp.max(jnp.where(valid, s, NEG), axis=-1, keepdims=True)
    e = jnp.where(valid, jnp.exp(s - m), 0.0)
    return e / jnp.maximum(jnp.sum(e, axis=-1, keepdims=True), jnp.finfo(F32).tiny)


def _nsa_attn_t_body(qh_ref, ql_ref, kch_ref, kcl_ref, lc_ref, ks_ref, vst_ref,
                     kw0, kw1, kw2, kw3, kw4, vw0, vw1, vw2, vw3, vw4,
                     g_ref, bw_ref, o_ref, mb_ref, sa_ref, sb_ref, *, n_cmp):
    QB = Q_BLOCK
    R = NSA_REP
    d = NSA_HEAD_DIM
    TB = SEL_TILE_BLOCKS
    KT = TB * SEL_LEN
    VROWS = NSA_VROWS
    q0 = pl.program_id(2) * QB
    tiny = jnp.finfo(F32).tiny

    def lanes4(x):
        return jnp.concatenate([x] * R, axis=1)

    def col_reduce(x, op):
        return op(x, axis=0, keepdims=True)

    def rhs(q_t, extra=None):
        parts = [jnp.concatenate([q_t[r * d:(r + 1) * d, :] for r in range(R)], axis=1)]
        if extra is not None:
            parts.append(extra)
        n = sum(x.shape[0] for x in parts)
        return jnp.concatenate(parts + [jnp.zeros((LANES - n, R * QB), BF16)], axis=0)

    qh_t = qh_ref[...]
    rhs_h = rhs(qh_t)
    rhs_l = rhs(ql_ref[...])

    NC = kch_ref.shape[2]
    NB = lc_ref.shape[2] - VROWS

    def compressed(nc):
        kch = kch_ref[0, 0, 0:nc, :]
        n_c = lax.broadcasted_iota(jnp.int32, (nc, QB), 0)
        t_c = q0 + lax.broadcasted_iota(jnp.int32, (nc, QB), 1)
        bias_c = jnp.where(n_c * CMP_STRIDE + (CMP_LEN - 1) <= t_c, jnp.where(n_c < n_cmp, 0.0, NEG), NEG)
        s_c = _dot(kch, rhs_h) + _dot(kcl_ref[0, 0, 0:nc, :], rhs_h) + _dot(kch, rhs_l) + lanes4(bias_c)
        m_c = jnp.maximum(col_reduce(s_c, jnp.max), 0.1 * NEG)
        e_c = jnp.exp2(s_c - m_c)
        oi = _dot(lc_ref[0, 0, :, 0:nc], e_c.astype(BF16))
        r_c = 1.0 / jnp.maximum(oi[d:d + 1], tiny)
        imp = oi[VROWS:] * r_c
        return oi[0:d] * r_c, sum(imp[:, r * QB:(r + 1) * QB] for r in range(R))

    if NC % 512 == 0:
        quarters = [NC * k // 4 for k in (1, 2, 3)]
        last_q = q0 + QB - 1
        needed = sum((nq * CMP_STRIDE + (CMP_LEN - 1) <= last_q).astype(jnp.int32) for nq in quarters)
        o_c, imp = lax.switch(needed, [functools.partial(compressed, nq) for nq in quarters + [NC]])
    else:
        o_c, imp = compressed(NC)

    blk_i = lax.broadcasted_iota(jnp.int32, (NB, QB), 0)
    blk = blk_i.astype(F32)
    t_b = q0 + lax.broadcasted_iota(jnp.int32, (NB, QB), 1)
    cur = t_b // SEL_LEN
    forced = (blk_i == 0) | (blk_i == cur) | (blk_i == cur - 1)
    mbias = jnp.where(forced, 0.0, NEG)
    score = jnp.where(forced, NEG, jnp.where(blk_i * SEL_LEN <= t_b, imp, NEG))
    for _ in range(min(SEL_TOPK, NB) - 3):
        mx = jnp.max(score, axis=0, keepdims=True)
        first = jnp.min(jnp.where(score == mx, blk, float(NB)), axis=0, keepdims=True)
        pick = blk == first
        mbias = jnp.where(pick, 0.0, mbias)
        score = jnp.where(pick, NEG, score)
    mb_ref[0:NB, :] = mbias
    if NB < mb_ref.shape[0]:
        mb_ref[NB:, :] = jnp.zeros((mb_ref.shape[0] - NB, QB), F32)

    def scores(j):
        k0 = pl.multiple_of(j * KT, KT)
        mb = mb_ref[pl.ds(pl.multiple_of(j * TB, TB), TB), :].astype(BF16)
        return _dot(ks_ref[0, pl.ds(k0, KT), :], rhs(qh_t, lanes4(mb)))

    def update(j, s, stats):
        m_old, acc = stats
        k0 = pl.multiple_of(j * KT, KT)
        m_new = jnp.maximum(m_old, col_reduce(s, jnp.max))
        alpha = jnp.exp2(m_old - m_new)
        pexp = jnp.exp2(s - m_new)
        return m_new, alpha * acc + _dot(vst_ref[0, :, pl.ds(k0, KT)], pexp.astype(BF16))

    def pair(p, stats):
        sb_ref[...] = scores(2 * p + 1)
        stats = update(2 * p, sa_ref[...], stats)
        sa_ref[...] = scores(2 * p + 2)
        return update(2 * p + 1, sb_ref[...], stats)

    def diag(j, s, stats):
        key = j * KT + lax.broadcasted_iota(jnp.int32, (KT, QB), 0)
        t_s = q0 + lax.broadcasted_iota(jnp.int32, (KT, QB), 1)
        return update(j, s + lanes4(jnp.where(key <= t_s, 0.0, NEG)), stats)

    n_full = q0 // KT
    init = (jnp.full((1, R * QB), NEG, F32), jnp.zeros((VROWS, R * QB), F32))
    sa_ref[...] = scores(0)
    stats = lax.fori_loop(0, n_full // 2, pair, init)
    j_even = (n_full // 2) * 2

    def odd_tail(stats):
        sb_ref[...] = scores(j_even + 1)
        stats = update(j_even, sa_ref[...], stats)
        return diag(j_even + 1, sb_ref[...], stats)

    _, acc_s = lax.cond(n_full % 2 == 1, odd_tail, lambda st: diag(j_even, sa_ref[...], st), stats)

    kwin = jnp.concatenate([kw0[0], kw1[0], kw2[0], kw3[0], kw4[0]], axis=0)
    vwin = jnp.concatenate([vw0[0], vw1[0], vw2[0], vw3[0], vw4[0]], axis=1)
    NW = WINDOW + QB
    kpos = q0 - WINDOW + lax.broadcasted_iota(jnp.int32, (NW, QB), 0)
    s_w = _dot(kwin, rhs_h) + lanes4(bw_ref[...] + jnp.where(kpos >= 0, 0.0, NEG))
    e_w = jnp.exp2(s_w - col_reduce(s_w, jnp.max))
    o_w = _dot(vwin, e_w.astype(BF16))

    def gate(c):
        return jnp.concatenate([g_ref[0, c, 0, r:r + 1, :] for r in range(R)], axis=1)

    o_t = (gate(0) * o_c + (gate(1) / acc_s[d:d + 1]) * acc_s[0:d]
           + (gate(2) / o_w[d:d + 1]) * o_w[0:d])
    o_ref[0] = jnp.concatenate(
        [jnp.concatenate([o_t[:, (2 * c) * QB:(2 * c + 1) * QB], o_t[:, (2 * c + 1) * QB:(2 * c + 2) * QB]],
                         axis=0).T for c in range(R // 2)], axis=1)


def nsa_attention_t(qh_t, ql_t, kch, kcl, vc_t, ks, vs_t, kw, vw_t, gates_t, batch, seq):
    W = qh_t.shape[0]
    G, R, d = NSA_GROUPS, NSA_REP, NSA_HEAD_DIM
    RW = R * d
    n_chunk = kch.shape[2]
    qpb = seq // Q_BLOCK
    wpb = WINDOW // Q_BLOCK
    n_cmp = n_chunk - CMP_LEN // CMP_STRIDE + 1
    n_sel = seq // SEL_LEN
    n = np.arange(n_chunk)[None, :]
    j = np.arange(n_sel)[:, None]
    overlap_t = ((n * CMP_STRIDE <= j * SEL_LEN + SEL_LEN - 1) & (n * CMP_STRIDE + CMP_LEN - 1 >= j * SEL_LEN)
                 & (n < n_cmp)).astype(np.float32)
    lc = jnp.concatenate([vc_t, jnp.broadcast_to(jnp.asarray(overlap_t, BF16), (batch, G, n_sel, n_chunk))],
                         axis=2)
    d = NSA_VROWS
    dist = np.arange(Q_BLOCK)[None, :] + WINDOW - np.arange(WINDOW + Q_BLOCK)[:, None]
    bias_w = np.where((dist >= 0) & (dist < WINDOW), 0.0, NEG).astype(np.float32)
    mb_rows = max(n_sel, SEL_TILE_BLOCKS)
    qblk = pl.BlockSpec((RW, Q_BLOCK), lambda b, g, i: (g, b * qpb + i))
    cblk = pl.BlockSpec((1, 1, n_chunk, LANES), lambda b, g, i: (b, g, 0, 0))
    wpos = lambda b, i, o: b * qpb + jnp.maximum(i + o - wpb, 0)
    kwblk = lambda o: pl.BlockSpec((1, Q_BLOCK, LANES), lambda b, g, i: (g, wpos(b, i, o), 0))
    vwblk = lambda o: pl.BlockSpec((1, d, Q_BLOCK), lambda b, g, i: (g, 0, wpos(b, i, o)))
    return pl.pallas_call(
        functools.partial(_nsa_attn_t_body, n_cmp=n_cmp),
        grid=(batch, G, qpb),
        in_specs=[qblk, qblk, cblk, cblk,
                  pl.BlockSpec((1, 1, d + n_sel, n_chunk), lambda b, g, i: (b, g, 0, 0)),
                  pl.BlockSpec((1, seq, LANES), lambda b, g, i: (g, b, 0)),
                  pl.BlockSpec((1, d, seq), lambda b, g, i: (g, 0, b))]
                 + [kwblk(o) for o in range(5)] + [vwblk(o) for o in range(5)]
                 + [pl.BlockSpec((1, 3, 1, R, Q_BLOCK), lambda b, g, i: (b, 0, g, 0, i)),
                    _full((WINDOW + Q_BLOCK, Q_BLOCK))],
        out_specs=pl.BlockSpec((1, Q_BLOCK, RW), lambda b, g, i: (b, i, g)),
        out_shape=jax.ShapeDtypeStruct((batch, seq, W), F32),
        scratch_shapes=[pltpu.VMEM((mb_rows, Q_BLOCK), F32)]
                       + [pltpu.VMEM((SEL_TILE_BLOCKS * SEL_LEN, R * Q_BLOCK), F32)] * 2,
        compiler_params=_cparams("parallel", "parallel", "arbitrary"),
        name="nsa_attention",
    )(qh_t, ql_t, kch, kcl, lc, ks, vs_t, *([kw] * 5), *([vw_t] * 5), gates_t, jnp.asarray(bias_w))


def nsa_mix_t(z_rest3, p, cos, sin):
    batch, seq, _ = z_rest3.shape
    G, R, d = NSA_GROUPS, NSA_REP, NSA_HEAD_DIM
    W = NSA_HEADS * d
    KW = G * d
    z2 = z_rest3.reshape(batch * seq, -1)
    qh_t, ql_t, ks, kw, vs_t, vw_t, gsig = nsa_prep(z2, p, seq, cos, sin)
    base = 2 * SEG + W
    raw = lambda i: z_rest3[..., base + i * KW:base + (i + 1) * KW]
    kch, kcl, vc = nsa_compress(raw(0), raw(1), p, batch, seq)
    n_chunk = vc.shape[2]
    ones = jnp.zeros((NSA_VROWS - d, n_chunk), BF16).at[0].set(1.0)
    vc_t = jnp.concatenate([vc[..., :d].transpose(0, 1, 3, 2),
                            jnp.broadcast_to(ones, (batch, G, NSA_VROWS - d, n_chunk))], axis=2)
    gates_t = (gsig[:, S_NSA:S_NSA + 3 * NSA_HEADS].reshape(batch, seq, G, R, 3).transpose(0, 4, 2, 3, 1))
    return nsa_attention_t(qh_t, ql_t, kch, kcl, vc_t, ks, vs_t, kw, vw_t, gates_t, batch, seq)


def _merge_body(h_ref, zg_ref, ya_ref, yb_ref, yc_ref, pa_ref, pb_ref, pc_ref, wo_ref, o_ref):
    D = D_MODEL

    def branch(i, y_ref, p_ref):
        gate = jax.nn.sigmoid(zg_ref[:, i * D:(i + 1) * D].astype(F32))
        return gate * _dot(y_ref[...].astype(BF16), p_ref[...])

    merged = branch(0, ya_ref, pa_ref) + branch(1, yb_ref, pb_ref) + branch(2, yc_ref, pc_ref)
    o_ref[...] = h_ref[...] + _dot(merged.astype(BF16), wo_ref[...])


def merge_out(h, z_gate, y_rw, y_gla, y_nsa, p, tm=256):
    T, D = h.shape
    W = y_rw.shape[1]
    row = lambda w: pl.BlockSpec((tm, w), lambda i: (i, 0))
    return pl.pallas_call(
        _merge_body,
        grid=(T // tm,),
        in_specs=[row(D), row(3 * D), row(W), row(W), row(W),
                  _full((W, D)), _full((W, D)), _full((W, D)), _full((D, D))],
        out_specs=row(D),
        out_shape=jax.ShapeDtypeStruct((T, D), F32),
        compiler_params=_cparams("parallel"),
        name="merge_out",
    )(h, z_gate, y_rw, y_gla, y_nsa, p['proj_a'].astype(BF16), p['proj_b'].astype(BF16),
      p['proj_c'].astype(BF16), p['w_out'].astype(BF16))


def _ffn_body(h_ref, g_ref, wg_ref, wu_ref, wd_ref, o_ref, xn_ref, acc_ref):
    f = pl.program_id(1)

    @pl.when(f == 0)
    def _():
        x = h_ref[...]
        ms = jnp.mean(x * x, axis=-1, keepdims=True)
        xn_ref[...] = (x * lax.rsqrt(ms + NORM_EPS) * g_ref[...]).astype(BF16)
        acc_ref[...] = jnp.zeros_like(acc_ref)

    xn = xn_ref[...]
    a = jax.nn.silu(_dot(xn, wg_ref[...])) * _dot(xn, wu_ref[...])
    acc_ref[...] += _dot(a.astype(BF16), wd_ref[...])

    @pl.when(f == pl.num_programs(1) - 1)
    def _():
        o_ref[...] = h_ref[...] + acc_ref[...]


def ffn_dense(h, p, tm=512, tf=512):
    T, D = h.shape
    F0 = p['ffn_gate'].shape[1]
    F = -(-F0 // tf) * tf
    cols = lambda w: jnp.pad(cast_bf16(w), ((0, 0), (0, F - F0)))
    w_gate, w_up = cols(p['ffn_gate']), cols(p['ffn_up'])
    w_down = jnp.pad(cast_bf16(p['ffn_down']), ((0, F - F0), (0, 0)))
    return pl.pallas_call(
        _ffn_body,
        grid=(T // tm, F // tf),
        in_specs=[pl.BlockSpec((tm, D), lambda i, f: (i, 0)), _full((1, D)),
                  pl.BlockSpec((D, tf), lambda i, f: (0, f)), pl.BlockSpec((D, tf), lambda i, f: (0, f)),
                  pl.BlockSpec((tf, D), lambda i, f: (f, 0))],
        out_specs=pl.BlockSpec((tm, D), lambda i, f: (i, 0)),
        out_shape=jax.ShapeDtypeStruct((T, D), F32),
        scratch_shapes=[pltpu.VMEM((tm, D), BF16), pltpu.VMEM((tm, D), F32)],
        compiler_params=_cparams("parallel", "arbitrary"),
        name="ffn_dense",
    )(h, p['norm_ffn'].reshape(1, D), w_gate, w_up, w_down)


def _router_body(h_ref, g_ref, wr_ref, xn_ref, route_ref):
    x = h_ref[...]
    ms = jnp.mean(x * x, axis=-1, keepdims=True)
    xn = x * lax.rsqrt(ms + NORM_EPS) * g_ref[...]
    xn_ref[...] = xn.astype(BF16)
    logits = _dot(xn, wr_ref[...], HI)
    lane = lax.broadcasted_iota(jnp.int32, logits.shape, 1)
    valid = lane < N_EXPERTS
    probs = _masked_softmax(logits, valid)
    p1 = jnp.max(probs, axis=-1, keepdims=True)
    i1 = jnp.min(jnp.where(valid & (probs == p1), lane, LANES), axis=-1, keepdims=True)
    rest = jnp.where(valid & (lane != i1), probs, -1.0)
    p2 = jnp.max(rest, axis=-1, keepdims=True)
    i2 = jnp.min(jnp.where(rest == p2, lane, LANES), axis=-1, keepdims=True)
    tot = p1 + p2
    route_ref[...] = jnp.where(lane == 0, i1.astype(F32),
                               jnp.where(lane == 1, i2.astype(F32),
                                         jnp.where(lane == 2, p1 / tot, jnp.where(lane == 3, p2 / tot, 0.0))))


def moe_router(h, p, tm=512):
    T, D = h.shape
    wr = jnp.zeros((D, LANES), F32).at[:, :N_EXPERTS].set(p['router'])
    return pl.pallas_call(
        _router_body,
        grid=(T // tm,),
        in_specs=[pl.BlockSpec((tm, D), lambda i: (i, 0)), _full((1, D)), _full((D, LANES))],
        out_specs=[pl.BlockSpec((tm, D), lambda i: (i, 0)), pl.BlockSpec((tm, LANES), lambda i: (i, 0))],
        out_shape=[jax.ShapeDtypeStruct((T, D), BF16), jax.ShapeDtypeStruct((T, LANES), F32)],
        compiler_params=_cparams("parallel"),
        name="moe_router",
    )(h, p['norm_ffn'].reshape(1, D), wr)


def _moe_body(te_ref, tv_ref, x_ref, wg_ref, wu_ref, wd_ref, o_ref, acc_ref):
    t = pl.program_id(0)
    f = pl.program_id(1)

    @pl.when(f == 0)
    def _():
        acc_ref[...] = jnp.zeros_like(acc_ref)

    @pl.when(tv_ref[t] > 0)
    def _():
        x = x_ref[...]
        a = jax.nn.silu(_dot(x, wg_ref[0])) * _dot(x, wu_ref[0])
        acc_ref[...] += _dot(a.astype(BF16), wd_ref[0])

    @pl.when(f == pl.num_programs(1) - 1)
    def _():
        o_ref[...] = acc_ref[...].astype(o_ref.dtype)


def moe_grouped(x_sorted, tile_expert, tile_valid, p, tm, tf=1024):
    P, D = x_sorted.shape
    F = p['exp_gate'].shape[2]
    nf = F // tf

    def fidx(f, tv, t):
        return jnp.where(tv[t] > 0, f, nf - 1)

    grid_spec = pltpu.PrefetchScalarGridSpec(
        num_scalar_prefetch=2,
        grid=(P // tm, nf),
        in_specs=[pl.BlockSpec((tm, D), lambda t, f, te, tv: (t, 0)),
                  pl.BlockSpec((1, D, tf), lambda t, f, te, tv: (te[t], 0, fidx(f, tv, t))),
                  pl.BlockSpec((1, D, tf), lambda t, f, te, tv: (te[t], 0, fidx(f, tv, t))),
                  pl.BlockSpec((1, tf, D), lambda t, f, te, tv: (te[t], fidx(f, tv, t), 0))],
        out_specs=pl.BlockSpec((tm, D), lambda t, f, te, tv: (t, 0)),
        scratch_shapes=[pltpu.VMEM((tm, D), F32)],
    )
    return pl.pallas_call(
        _moe_body,
        grid_spec=grid_spec,
        out_shape=jax.ShapeDtypeStruct((P, D), BF16),
        compiler_params=_cparams("arbitrary", "arbitrary"),
        name="moe_grouped",
    )(tile_expert, tile_valid, x_sorted, cast_bf16(p['exp_gate']), cast_bf16(p['exp_up']),
      cast_bf16(p['exp_down']))


def _combine_body(h_ref, ya_ref, yb_ref, route_ref, o_ref):
    w = route_ref[...]
    o_ref[...] = h_ref[...] + w[:, 2:3] * ya_ref[...].astype(F32) + w[:, 3:4] * yb_ref[...].astype(F32)


def moe_combine(h, ya, yb, route, tm=512):
    T, D = h.shape
    row = lambda w: pl.BlockSpec((tm, w), lambda i: (i, 0))
    return pl.pallas_call(
        _combine_body,
        grid=(T // tm,),
        in_specs=[row(D), row(D), row(D), row(LANES)],
        out_specs=row(D),
        out_shape=jax.ShapeDtypeStruct((T, D), F32),
        compiler_params=_cparams("parallel"),
        name="moe_combine",
    )(h, ya, yb, route)


def moe_ffn(h, p, tm=512):
    T, D = h.shape
    xn, route = moe_router(h, p)
    experts = route[:, 0:TOP_K].astype(jnp.int32).reshape(-1)
    n_pairs = T * TOP_K
    n_tiles = n_pairs // tm + N_EXPERTS
    eid = jnp.arange(N_EXPERTS, dtype=jnp.int32)
    onehot = (experts[:, None] == eid[None, :]).astype(jnp.int32)
    running = jnp.cumsum(onehot, axis=0)
    counts = running[-1]
    tiles_e = (counts + tm - 1) // tm
    tile_end = jnp.cumsum(tiles_e)
    row_start = (tile_end - tiles_e) * tm
    pair_start = jnp.cumsum(counts) - counts
    dest = jnp.sum(onehot * (row_start[None, :] + running - 1), axis=1)
    order = jnp.argsort(experts, stable=True).astype(jnp.int32)
    tiles = jnp.arange(n_tiles, dtype=jnp.int32)
    tile_valid = (tiles < tile_end[-1]).astype(jnp.int32)
    tile_expert = jnp.minimum(jnp.sum((tiles[:, None] >= tile_end[None, :]).astype(jnp.int32), axis=1),
                              N_EXPERTS - 1)
    last_e = jnp.minimum(jnp.sum((tile_end[-1] - 1 >= tile_end).astype(jnp.int32)), N_EXPERTS - 1)
    tile_expert = jnp.where(tile_valid > 0, tile_expert, last_e)
    row_oh = (jnp.repeat(tile_expert, tm)[:, None] == eid[None, :]).astype(jnp.int32)
    k_in_e = jnp.arange(n_tiles * tm, dtype=jnp.int32) - jnp.sum(row_oh * row_start[None, :], axis=1)
    row_ok = (k_in_e < jnp.sum(row_oh * counts[None, :], axis=1)) & (jnp.repeat(tile_valid, tm) > 0)
    src = jnp.clip(jnp.sum(row_oh * pair_start[None, :], axis=1) + k_in_e, 0, n_pairs - 1)
    row_token = jnp.where(row_ok, order[src] // TOP_K, 0)
    y_sorted = moe_grouped(xn[row_token], tile_expert, tile_valid, p, tm)
    dest = dest.reshape(T, TOP_K)
    return moe_combine(h, y_sorted[dest[:, 0]], y_sorted[dest[:, 1]], route)


def _reorder_w_in(w_in, has_vres):
    D = D_MODEL
    n_rw = 3 * RW_WIDTH + 160 + (32 if has_vres else 0)
    rw0 = 3 * D
    gla0 = rw0 + n_rw
    nsa0 = gla0 + 1552
    col = lambda a, b: w_in[:, a:b]
    zeros = lambda n: jnp.zeros((D, n), w_in.dtype)
    small = [col(rw0 + 1536, rw0 + n_rw)] + ([] if has_vres else [zeros(32)])
    small += [col(gla0 + 1024, gla0 + 1040), col(nsa0 + 1280, nsa0 + 1304), zeros(SMALL - S_NSA - 24)]
    rest = [col(rw0, rw0 + 1536), col(gla0, gla0 + 1024), col(gla0 + 1040, gla0 + 1552),
            col(nsa0, nsa0 + 1280)] + small
    return w_in[:, :rw0].astype(BF16), jnp.concatenate(rest, axis=1).astype(BF16)


def _layer(h, p, v_first, batch, seq, cos, sin, is_moe):
    T = batch * seq
    w_gate, w_rest = _reorder_w_in(p['w_in'], v_first is not None)
    z_gate = norm_matmul(h, p['norm_mix'], w_gate, BF16, 1024, 1536)
    z_rest = norm_matmul(h, p['norm_mix'], w_rest, F32, 1024, 1536)
    z_rest3 = z_rest.reshape(batch, seq, -1)
    prep = rwkv_prep(z_rest, v_first, p, seq)
    if v_first is None:
        v_first = prep[3]
    y_rw = rwkv_chunk(prep, p, batch, seq).reshape(T, -1)
    y_gla = gla_mix(z_rest3, p).reshape(T, -1)
    y_nsa = nsa_mix_t(z_rest3, p, cos, sin).reshape(T, -1)
    h = merge_out(h, z_gate, y_rw, y_gla, y_nsa, p)
    h = moe_ffn(h, p) if is_moe else ffn_dense(h, p)
    return h, v_first


_L0 = ('norm_mix', 'w_in', 'rw_mu', 'rw_w0', 'rw_w2', 'rw_a0', 'rw_a2', 'rw_g2', 'rw_kk', 'rw_ka', 'rw_rk',
       'rw_ln_w', 'rw_ln_b', 'gla_wg', 'gla_bg', 'gla_norm', 'nsa_qnorm', 'nsa_kcnorm', 'nsa_ksnorm',
       'nsa_kwnorm', 'nsa_pe_k', 'nsa_c1_k', 'nsa_c2_k', 'nsa_pe_v', 'nsa_c1_v', 'nsa_c2_v', 'proj_a', 'proj_b',
       'proj_c', 'w_out', 'norm_ffn', 'ffn_gate', 'ffn_up', 'ffn_down')
_L1 = ('norm_mix', 'w_in', 'rw_mu', 'rw_w0', 'rw_w2', 'rw_a0', 'rw_a2', 'rw_g2', 'rw_v0', 'rw_v2', 'rw_kk',
       'rw_ka', 'rw_rk', 'rw_ln_w', 'rw_ln_b', 'gla_wg', 'gla_bg', 'gla_norm', 'nsa_qnorm', 'nsa_kcnorm',
       'nsa_ksnorm', 'nsa_kwnorm', 'nsa_pe_k', 'nsa_c1_k', 'nsa_c2_k', 'nsa_pe_v', 'nsa_c1_v', 'nsa_c2_v',
       'proj_a', 'proj_b', 'proj_c', 'w_out', 'norm_ffn', 'router', 'exp_gate', 'exp_up', 'exp_down')


def kernel(x, *weights):
    p0 = dict(zip(_L0, weights[:len(_L0)]))
    p1 = dict(zip(_L1, weights[len(_L0):]))
    batch, seq, D = x.shape
    cos, sin = _rope_tables(jnp.arange(seq), NSA_HEADS * NSA_HEAD_DIM)
    h = x.reshape(batch * seq, D)
    h, v_first = _layer(h, p0, None, batch, seq, cos, sin, False)
    h, _ = _layer(h, p1, v_first, batch, seq, cos, sin, True)
    return h.reshape(batch, seq, D)
```
